```python
import jax, jax.numpy as jnp
from jax import lax
import numpy as np

D_MODEL = 1024
BATCH = 8
SEQ = 2048
DEPTH = 4
DEC_BATCH = 128
DEC_SEQ = 4
PAST_LEN = 2048
PAGE_SIZE = 128

HEAD_DIM = 64
N_HEADS_A = 8
N_HEADS_B = 8
N_HEADS = N_HEADS_A + N_HEADS_B
WIDTH_A = N_HEADS_A * HEAD_DIM
WIDTH_B = N_HEADS_B * HEAD_DIM
MOBA_BLOCK = 256
MOBA_TOPK = 3
Q_BLOCK = 128
D_FF = 2816
N_EXPERTS = 8
TOP_K = 2
D_FF_EXPERT = 3584
N_DENSE = (DEPTH + 1) // 2
N_MOE = DEPTH // 2
ALPHA = (2.0 * DEPTH) ** 0.25
BETA = (8.0 * DEPTH) ** -0.25
LN_EPS = 1e-5
SCALE = HEAD_DIM ** -0.5
NEG_INF = float('-inf')
OFF_QA = 0
OFF_KA = OFF_QA + WIDTH_A
OFF_VA = OFF_KA + WIDTH_A
OFF_QB = OFF_VA + WIDTH_A
OFF_KB = OFF_QB + WIDTH_B
OFF_VB = OFF_KB + WIDTH_B
OFF_F = OFF_VB + WIDTH_B
OFF_GA = OFF_F + N_HEADS_B
OFF_GB = OFF_GA + D_MODEL
N_IN = OFF_GB + D_MODEL

kernel_name = 'moba_fox_gated_hybrid_step'


def alibi_slopes():
    return 2.0 ** (-8.0 * jnp.arange(1, N_HEADS_A + 1, dtype=jnp.float32) / N_HEADS_A)


def layer_norm(x, g, b):
    x32 = x.astype(jnp.float32)
    mu = jnp.mean(x32, axis=-1, keepdims=True)
    var = jnp.mean(jnp.square(x32 - mu), axis=-1, keepdims=True)
    return ((x32 - mu) * lax.rsqrt(var + LN_EPS) * g + b).astype(x.dtype)


def project_in(x, w_in_l, b_f_l):
    lead = x.shape[:-1]
    z = x @ w_in_l
    def heads(lo, n):
        return z[..., lo:lo + n * HEAD_DIM].reshape(*lead, n, HEAD_DIM)
    qa, ka, va = heads(OFF_QA, N_HEADS_A), heads(OFF_KA, N_HEADS_A), heads(OFF_VA, N_HEADS_A)
    qb, kb, vb = heads(OFF_QB, N_HEADS_B), heads(OFF_KB, N_HEADS_B), heads(OFF_VB, N_HEADS_B)
    logf = jax.nn.log_sigmoid(z[..., OFF_F:OFF_GA].astype(jnp.float32) + b_f_l.astype(jnp.float32))
    return qa, ka, va, qb, kb, vb, logf, z[..., OFF_GA:OFF_GB], z[..., OFF_GB:]


def merge_out(oa, ob, ga, gb, w_pa_l, w_pb_l, w_o_l):
    lead = oa.shape[:-2]
    ya = oa.reshape(*lead, WIDTH_A) @ w_pa_l
    yb = ob.reshape(*lead, WIDTH_B) @ w_pb_l
    return (jax.nn.sigmoid(ga) * ya + jax.nn.sigmoid(gb) * yb) @ w_o_l


def moba_prompt(q, k, v, slopes):
    B, T, H, _ = q.shape
    nb = -(-T // MOBA_BLOCK)
    nq = T // Q_BLOCK
    n_sel = min(MOBA_TOPK, nb - 1)
    pad = nb * MOBA_BLOCK - T
    padw = ((0, 0), (0, pad), (0, 0), (0, 0))
    kbt = jnp.pad(k, padw).reshape(B, nb, MOBA_BLOCK, H, HEAD_DIM).transpose(0, 3, 1, 2, 4)
    vbt = jnp.pad(v, padw).reshape(B, nb, MOBA_BLOCK, H, HEAD_DIM).transpose(0, 3, 1, 2, 4)
    kmean = jnp.mean(kbt.astype(jnp.float32), axis=3)
    qs = q.reshape(B * nq, Q_BLOCK, H, HEAD_DIM)
    b_ids = jnp.repeat(jnp.arange(B), nq)
    q_ids = jnp.tile(jnp.arange(nq), B)
    ar_blk = jnp.arange(MOBA_BLOCK)

    def block(args):
        qc, bi, qi = args
        qh = qc.transpose(1, 0, 2)
        kb_b, vb_b = kbt[bi], vbt[bi]
        own = (qi * Q_BLOCK) // MOBA_BLOCK
        pos_q = qi * Q_BLOCK + jnp.arange(Q_BLOCK)
        k_own = lax.dynamic_index_in_dim(kb_b, own, axis=1, keepdims=False)
        v_own = lax.dynamic_index_in_dim(vb_b, own, axis=1, keepdims=False)
        pos_own = own * MOBA_BLOCK + ar_blk
        s_own = (jnp.einsum('hqd,hjd->hqj', qh, k_own, preferred_element_type=jnp.float32) * SCALE
                 - slopes[:, None, None] * (pos_q[:, None] - pos_own[None, :]).astype(jnp.float32)[None])
        s_own = jnp.where((pos_own[None, :] <= pos_q[:, None])[None], s_own, NEG_INF)
        if n_sel == 0:
            p = jax.nn.softmax(s_own, axis=-1).astype(v.dtype)
            out = jnp.einsum('hqj,hjd->hqd', p, v_own)
        else:
            gate = jnp.einsum('hqd,hnd->hqn', qh.astype(jnp.float32), kmean[bi])
            gate = jnp.where(jnp.arange(nb)[None, None, :] < own, gate, NEG_INF)
            _, idx = lax.top_k(gate, n_sel)
            k_sel = jax.vmap(lambda kh, ih: kh[ih])(kb_b, idx)
            v_sel = jax.vmap(lambda vh, ih: vh[ih])(vb_b, idx)
            pos_sel = idx[..., None] * MOBA_BLOCK + ar_blk
            s_sel = (jnp.einsum('hqd,hqnjd->hqnj', qh, k_sel, preferred_element_type=jnp.float32) * SCALE
                     - slopes[:, None, None, None] * (pos_q[None, :, None, None] - pos_sel).astype(jnp.float32))
            s_sel = jnp.where((jnp.arange(n_sel) < own)[None, None, :, None], s_sel, NEG_INF)
            s = jnp.concatenate([s_sel.reshape(H, Q_BLOCK, n_sel * MOBA_BLOCK), s_own], axis=-1)
            p = jax.nn.softmax(s, axis=-1).astype(v.dtype)
            p_sel = p[..., :n_sel * MOBA_BLOCK].reshape(H, Q_BLOCK, n_sel, MOBA_BLOCK)
            out = (jnp.einsum('hqnj,hqnjd->hqd', p_sel, v_sel)
                   + jnp.einsum('hqj,hjd->hqd', p[..., n_sel * MOBA_BLOCK:], v_own))
        return out.transpose(1, 0, 2)

    out = lax.map(block, (qs, b_ids, q_ids))
    return out.reshape(B, T, H, HEAD_DIM)


def fox_prompt(q, k, v, logf):
    B, T, H, _ = q.shape
    nq = T // Q_BLOCK
    c = jnp.cumsum(logf, axis=1)
    cT = c.transpose(0, 2, 1)
    qs = q.reshape(B, nq, Q_BLOCK, H, HEAD_DIM).transpose(1, 0, 2, 3, 4)
    cs = c.reshape(B, nq, Q_BLOCK, H).transpose(1, 0, 3, 2)
    pos_k = jnp.arange(T)

    def block(args):
        qc, cq, i = args
        pos_q = i * Q_BLOCK + jnp.arange(Q_BLOCK)
        s = (jnp.einsum('bqhd,bkhd->bhqk', qc, k, preferred_element_type=jnp.float32) * SCALE
             + cq[..., None] - cT[:, :, None, :])
        s = jnp.where((pos_k[None, :] <= pos_q[:, None])[None, None], s, NEG_INF)
        p = jax.nn.softmax(s, axis=-1).astype(v.dtype)
        return jnp.einsum('bhqk,bkhd->bqhd', p, v)

    out = lax.map(block, (qs, cs, jnp.arange(nq)))
    return out.transpose(1, 0, 2, 3, 4).reshape(B, T, H, HEAD_DIM)


def attend_sample_one(cache_k, cache_v, cache_logf, l, slopes, qa, ka, va, qb, kb, vb, logf, pt):
    n_pages = pt.shape[0]
    past = n_pages * PAGE_SIZE
    S = qa.shape[0]
    ppb = MOBA_BLOCK // PAGE_SIZE
    own = past // MOBA_BLOCK
    n_own_past = past - own * MOBA_BLOCK
    n_sel = min(MOBA_TOPK, own)
    pos_q = past + jnp.arange(S)
    causal = jnp.arange(S)[None, :] <= jnp.arange(S)[:, None]
    f32 = jnp.float32

    qh = qa.transpose(1, 0, 2)
    s_list, v_list = [], []
    if n_sel > 0:
        k_full = cache_k[l, pt[:own * ppb], :, :N_HEADS_A, :].reshape(own, MOBA_BLOCK, N_HEADS_A, HEAD_DIM)
        kmean = jnp.mean(k_full.astype(f32), axis=1)
        gate = jnp.einsum('hsd,nhd->hsn', qh.astype(f32), kmean)
        _, idx = lax.top_k(gate, n_sel)
        phys = pt[idx[..., None] * ppb + jnp.arange(ppb)]
        hh = jnp.broadcast_to(jnp.arange(N_HEADS_A)[:, None, None, None], phys.shape)
        ll = jnp.full(phys.shape, l, dtype=phys.dtype)
        k_sel = cache_k[ll, phys, :, hh, :].reshape(N_HEADS_A, S, n_sel * MOBA_BLOCK, HEAD_DIM)
        v_sel = cache_v[ll, phys, :, hh, :].reshape(N_HEADS_A, S, n_sel * MOBA_BLOCK, HEAD_DIM)
        pos_sel = (idx[..., None] * MOBA_BLOCK + jnp.arange(MOBA_BLOCK)).reshape(N_HEADS_A, S, n_sel * MOBA_BLOCK)
        s = (jnp.einsum('hsd,hskd->hsk', qh, k_sel, preferred_element_type=f32) * SCALE
             - slopes[:, None, None] * (pos_q[None, :, None] - pos_sel).astype(f32))
        s_list.append(s)
        v_list.append(v_sel)
    if n_own_past > 0:
        pages = pt[own * ppb:]
        k_op = cache_k[l, pages, :, :N_HEADS_A, :].reshape(n_own_past, N_HEADS_A, HEAD_DIM)
        v_op = cache_v[l, pages, :, :N_HEADS_A, :].reshape(n_own_past, N_HEADS_A, HEAD_DIM)
        pos_op = own * MOBA_BLOCK + jnp.arange(n_own_past)
        s = (jnp.einsum('hsd,khd->hsk', qh, k_op, preferred_element_type=f32) * SCALE
             - slopes[:, None, None] * (pos_q[:, None] - pos_op[None, :]).astype(f32)[None])
        s_list.append(s)
        v_list.append(jnp.broadcast_to(v_op.transpose(1, 0, 2)[:, None], (N_HEADS_A, S, n_own_past, HEAD_DIM)))
    s = (jnp.einsum('hsd,khd->hsk', qh, ka, preferred_element_type=f32) * SCALE
         - slopes[:, None, None] * (pos_q[:, None] - pos_q[None, :]).astype(f32)[None])
    s_list.append(jnp.where(causal[None], s, NEG_INF))
    v_list.append(jnp.broadcast_to(va.transpose(1, 0, 2)[:, None], (N_HEADS_A, S, S, HEAD_DIM)))
    p = jax.nn.softmax(jnp.concatenate(s_list, axis=-1), axis=-1).astype(va.dtype)
    oa = jnp.einsum('hsk,hskd->shd', p, jnp.concatenate(v_list, axis=2))

    k_pb = cache_k[l, pt, :, N_HEADS_A:, :].reshape(past, N_HEADS_B, HEAD_DIM)
    v_pb = cache_v[l, pt, :, N_HEADS_A:, :].reshape(past, N_HEADS_B, HEAD_DIM)
    lf_p = cache_logf[l, pt].reshape(past, N_HEADS_B).astype(f32)
    r = lax.cumsum(lf_p, axis=0, reverse=True) - lf_p
    cn = jnp.cumsum(logf, axis=0)
    qbh = qb.transpose(1, 0, 2)
    s_p = (jnp.einsum('hsd,khd->hsk', qbh, k_pb, preferred_element_type=f32) * SCALE
           + cn.T[:, :, None] + r.T[:, None, :])
    s_n = (jnp.einsum('hsd,khd->hsk', qbh, kb, preferred_element_type=f32) * SCALE
           + cn.T[:, :, None] - cn.T[:, None, :])
    s_n = jnp.where(causal[None], s_n, NEG_INF)
    p = jax.nn.softmax(jnp.concatenate([s_p, s_n], axis=-1), axis=-1).astype(vb.dtype)
    ob = jnp.einsum('hsk,khd->shd', p[..., :past], v_pb) + jnp.einsum('hsk,khd->shd', p[..., past:], vb)
    return oa, ob


def swiglu(h, wg, wu, wd):
    return (jax.nn.silu(h @ wg) * (h @ wu)) @ wd


def moe_swiglu(h, w_r, b_r, wg, wu, wd):
    lead = h.shape[:-1]
    h2 = h.reshape(-1, D_MODEL)
    logits = jnp.matmul(h2, w_r, preferred_element_type=jnp.float32) + b_r.astype(jnp.float32)
    top_vals, top_idx = lax.top_k(logits, TOP_K)
    gates = jax.nn.softmax(top_vals, axis=-1)
    combine = jnp.sum(jax.nn.one_hot(top_idx, N_EXPERTS, dtype=jnp.float32) * gates[..., None], axis=1)
    out = jnp.zeros_like(h2)
    for e in range(N_EXPERTS):
        out = out + combine[:, e:e + 1].astype(h2.dtype) * swiglu(h2, wg[e], wu[e], wd[e])
    return out.reshape(*lead, D_MODEL)


def channel_mix(x, l, w_gate_d, w_up_d, w_down_d, w_router, b_router, w_gate_e, w_up_e, w_down_e):
    i = l // 2
    if l % 2 == 0:
        return swiglu(x, w_gate_d[i], w_up_d[i], w_down_d[i])
    return moe_swiglu(x, w_router[i], b_router[i], w_gate_e[i], w_up_e[i], w_down_e[i])


def setup_inputs(seed: int = 0) -> dict:
    key = jax.random.key(seed)
    ks = jax.random.split(key, 24)
    f32 = jnp.float32
    n_pages = PAST_LEN // PAGE_SIZE
    n_pool = (5 * DEC_BATCH * n_pages) // 4

    def nrm(k, shape, scale):
        return jax.random.normal(k, shape, f32) * scale

    col_scale = jnp.ones((N_IN,), f32).at[OFF_VA:OFF_QB].set(BETA).at[OFF_VB:OFF_F].set(BETA)
    page_table = jax.random.permutation(ks[5], n_pool)[:DEC_BATCH * n_pages].reshape(DEC_BATCH, n_pages).astype(jnp.int32)
    return {
        'x_prompt': nrm(ks[0], (BATCH, SEQ, D_MODEL), 1.0),
        'x_sample': nrm(ks[1], (DEC_BATCH, DEC_SEQ, D_MODEL), 1.0),
        'cache_k': nrm(ks[2], (DEPTH, n_pool, PAGE_SIZE, N_HEADS, HEAD_DIM), 1.0),
        'cache_v': nrm(ks[3], (DEPTH, n_pool, PAGE_SIZE, N_HEADS, HEAD_DIM), 1.0),
        'cache_logf': jax.nn.log_sigmoid(2.0 + jax.random.normal(ks[4], (DEPTH, n_pool, PAGE_SIZE, N_HEADS_B), f32)),
        'page_table': page_table,
        'w_in': nrm(ks[6], (DEPTH, D_MODEL, N_IN), D_MODEL ** -0.5) * col_scale,
        'b_f': 2.0 + nrm(ks[7], (DEPTH, N_HEADS_B), 0.5),
        'w_pa': nrm(ks[8], (DEPTH, WIDTH_A, D_MODEL), BETA * WIDTH_A ** -0.5),
        'w_pb': nrm(ks[9], (DEPTH, WIDTH_B, D_MODEL), BETA * WIDTH_B ** -0.5),
        'w_o': nrm(ks[10], (DEPTH, D_MODEL, D_MODEL), BETA * D_MODEL ** -0.5),
        'ln1_g': 1.0 + nrm(ks[11], (DEPTH, D_MODEL), 0.02),
        'ln1_b': nrm(ks[12], (DEPTH, D_MODEL), 0.02),
        'ln2_g': 1.0 + nrm(ks[13], (DEPTH, D_MODEL), 0.02),
        'ln2_b': nrm(ks[14], (DEPTH, D_MODEL), 0.02),
        'w_gate_d': nrm(ks[15], (N_DENSE, D_MODEL, D_FF), BETA * D_MODEL ** -0.5),
        'w_up_d': nrm(ks[16], (N_DENSE, D_MODEL, D_FF), BETA * D_MODEL ** -0.5),
        'w_down_d': nrm(ks[17], (N_DENSE, D_FF, D_MODEL), BETA * D_FF ** -0.5),
        'w_router': nrm(ks[18], (N_MOE, D_MODEL, N_EXPERTS), D_MODEL ** -0.5),
        'b_router': nrm(ks[19], (N_MOE, N_EXPERTS), 0.01),
        'w_gate_e': nrm(ks[20], (N_MOE, N_EXPERTS, D_MODEL, D_FF_EXPERT), BETA * D_MODEL ** -0.5),
        'w_up_e': nrm(ks[21], (N_MOE, N_EXPERTS, D_MODEL, D_FF_EXPERT), BETA * D_MODEL ** -0.5),
        'w_down_e': nrm(ks[22], (N_MOE, N_EXPERTS, D_FF_EXPERT, D_MODEL), BETA * D_FF_EXPERT ** -0.5),
    }


def reference(x_prompt, x_sample, cache_k, cache_v, cache_logf, page_table, w_in, b_f, w_pa, w_pb, w_o,
              ln1_g, ln1_b, ln2_g, ln2_b, w_gate_d, w_up_d, w_down_d, w_router, b_router,
              w_gate_e, w_up_e, w_down_e):
    slopes = alibi_slopes()
    xp, xs = x_prompt, x_sample
    kp, vp, lp, ksm, vsm, lsm = [], [], [], [], [], []
    for l in range(DEPTH):
        qa, ka, va, qb, kb, vb, logf, ga, gb = project_in(xp, w_in[l], b_f[l])
        oa = moba_prompt(qa, ka, va, slopes)
        ob = fox_prompt(qb, kb, vb, logf)
        xp = layer_norm(ALPHA * xp + merge_out(oa, ob, ga, gb, w_pa[l], w_pb[l], w_o[l]), ln1_g[l], ln1_b[l])
        kp.append(jnp.concatenate([ka, kb], axis=-2))
        vp.append(jnp.concatenate([va, vb], axis=-2))
        lp.append(logf)
        qa, ka, va, qb, kb, vb, logf, ga, gb = project_in(xs, w_in[l], b_f[l])
        oa, ob = lax.map(lambda a: attend_sample_one(cache_k, cache_v, cache_logf, l, slopes, *a),
                         (qa, ka, va, qb, kb, vb, logf, page_table))
        xs = layer_norm(ALPHA * xs + merge_out(oa, ob, ga, gb, w_pa[l], w_pb[l], w_o[l]), ln1_g[l], ln1_b[l])
        ksm.append(jnp.concatenate([ka, kb], axis=-2))
        vsm.append(jnp.concatenate([va, vb], axis=-2))
        lsm.append(logf)
        ffn_args = (w_gate_d, w_up_d, w_down_d, w_router, b_router, w_gate_e, w_up_e, w_down_e)
        xp = layer_norm(ALPHA * xp + channel_mix(xp, l, *ffn_args), ln2_g[l], ln2_b[l])
        xs = layer_norm(ALPHA * xs + channel_mix(xs, l, *ffn_args), ln2_g[l], ln2_b[l])
    k_prompt = jnp.stack(kp)
    v_prompt = jnp.stack(vp)
    logf_prompt = jnp.stack(lp)
    k_sample = jnp.stack(ksm)
    v_sample = jnp.stack(vsm)
    logf_sample = jnp.stack(lsm)
    return (xp, xs, k_prompt, v_prompt, logf_prompt, k_sample, v_sample, logf_sample)
```

```python
import functools

import jax
import jax.numpy as jnp
from jax import lax
from jax.experimental import pallas as pl
from jax.experimental.pallas import tpu as pltpu

F32 = jnp.float32
BF16 = jnp.bfloat16

HEAD_DIM = 64
N_HEADS_A = 8
N_HEADS_B = 8
N_HEADS = N_HEADS_A + N_HEADS_B
WIDTH_A = N_HEADS_A * HEAD_DIM
WIDTH_B = N_HEADS_B * HEAD_DIM
MOBA_BLOCK = 256
MOBA_TOPK = 3
TOP_K = 2
LN_EPS = 1e-5
SCALE = HEAD_DIM ** -0.5
MASKED = -1e30

LANES = 128
SUBLANES = 8
VMEM_LIMIT_BYTES = 56 * 1024 * 1024


def _params(*sem):
    return pltpu.CompilerParams(dimension_semantics=sem, vmem_limit_bytes=VMEM_LIMIT_BYTES)


def _pick_tile(n, cap):
    t = cap
    while t > SUBLANES and n % t:
        t //= 2
    assert n % t == 0, (n, cap)
    return t


def _dot(a, b):
    return jnp.dot(a, b, preferred_element_type=F32)


def _dot_nt(a, b):
    return lax.dot_general(a, b, (((1,), (1,)), ((), ())), preferred_element_type=F32)


def _sigmoid(x):
    return 1.0 / (1.0 + jnp.exp(-x))


def _layer_norm(y, g, b):
    mu = jnp.mean(y, axis=-1, keepdims=True)
    d = y - mu
    var = jnp.mean(d * d, axis=-1, keepdims=True)
    return d * lax.rsqrt(var + LN_EPS) * g + b


def _lane_cumsum(x):
    n = x.shape[-1]
    lane = lax.broadcasted_iota(jnp.int32, x.shape, x.ndim - 1)
    sh = 1
    while sh < n:
        x = x + jnp.where(lane >= sh, pltpu.roll(x, sh, x.ndim - 1), 0.0)
        sh *= 2
    return x


def _proj_kernel(x_ref, wq_ref, wk_ref, wv_ref, wg_ref, wf_ref, bf_ref,
                 q_ref, k_ref, v_ref, g_ref, f_ref):
    x = x_ref[...].astype(BF16)
    q_ref[...] = _dot(x, wq_ref[...]).astype(BF16)
    k_ref[...] = _dot(x, wk_ref[...])
    v_ref[...] = _dot(x, wv_ref[...])
    g_ref[...] = _dot(x, wg_ref[...])
    z = _dot(x, wf_ref[...]) + bf_ref[...]
    f_ref[...] = jnp.minimum(z, 0.0) - jnp.log(1.0 + jnp.exp(-jnp.abs(z)))


def _project_in(x, wq, wk, wv, wg, wf, bf):
    n, d = x.shape
    tm = _pick_tile(n, 256)
    row = lambda i: (i, 0)
    fix = lambda i: (0, 0)
    return pl.pallas_call(
        _proj_kernel,
        grid=(n // tm,),
        in_specs=[pl.BlockSpec((tm, d), row)] + [pl.BlockSpec(w.shape, fix) for w in (wq, wk, wv, wg, wf, bf)],
        out_specs=[pl.BlockSpec((tm, wq.shape[1]), row), pl.BlockSpec((tm, wk.shape[1]), row),
                   pl.BlockSpec((tm, wv.shape[1]), row), pl.BlockSpec((tm, wg.shape[1]), row),
                   pl.BlockSpec((tm, LANES), row)],
        out_shape=[jax.ShapeDtypeStruct((n, wq.shape[1]), BF16), jax.ShapeDtypeStruct((n, wk.shape[1]), F32),
                   jax.ShapeDtypeStruct((n, wv.shape[1]), F32), jax.ShapeDtypeStruct((n, wg.shape[1]), F32),
                   jax.ShapeDtypeStruct((n, LANES), F32)],
        compiler_params=_params("parallel"),
        name="project_in",
    )(x, wq, wk, wv, wg, wf, bf)


def _cumsum_kernel(x_ref, o_ref):
    o_ref[...] = _lane_cumsum(x_ref[...])


def _cumsum_rows(x):
    return pl.pallas_call(
        _cumsum_kernel,
        out_shape=jax.ShapeDtypeStruct(x.shape, F32),
        compiler_params=_params(),
        name="logf_cumsum",
    )(x)


def _softmax_step(s, v, m, l, acc):
    m_new = jnp.maximum(m, jnp.max(s, axis=-1, keepdims=True))
    a = jnp.exp(m - m_new)
    p = jnp.exp(s - m_new)
    l = a * l + jnp.sum(p, axis=-1, keepdims=True)
    acc = a * acc + _dot(p.astype(BF16), v)
    return m_new, l, acc


def _fox_prompt_kernel(q_ref, k_ref, v_ref, c_ref, o_ref, kb_s, *, tq):
    qi = pl.program_id(2)

    @pl.when(qi == 0)
    def _():
        kb_s[...] = k_ref[...].astype(BF16)

    q = q_ref[...]
    row = lax.broadcasted_iota(jnp.int32, (tq, tq), 0)
    col = lax.broadcasted_iota(jnp.int32, (tq, tq), 1)

    def logits(j):
        start = pl.multiple_of(j * tq, tq)
        s = _dot_nt(q, kb_s[pl.ds(start, tq), :]) * SCALE - c_ref[pl.ds(j, 1), :]
        return s, v_ref[pl.ds(start, tq), :]

    def past(j, carry):
        s, v = logits(j)
        return _softmax_step(s, v, *carry)

    init = (jnp.full((tq, 1), MASKED, F32), jnp.zeros((tq, 1), F32), jnp.zeros((tq, HEAD_DIM), F32))
    carry = lax.fori_loop(0, qi, past, init)
    s, v = logits(qi)
    _, l, acc = _softmax_step(jnp.where(col <= row, s, MASKED), v, *carry)
    o_ref[...] = acc / l


def _fox_prompt(qt, kt, vt, c4):
    b, _, t, hd = qt.shape
    tq = c4.shape[-1]
    fox = lambda bi, h, qi: (bi, N_HEADS_A + h, qi, 0)
    fox_all = lambda bi, h, qi: (bi, N_HEADS_A + h, 0, 0)
    return pl.pallas_call(
        functools.partial(_fox_prompt_kernel, tq=tq),
        grid=(b, N_HEADS_B, t // tq),
        in_specs=[pl.BlockSpec((None, None, tq, hd), fox),
                  pl.BlockSpec((None, None, t, hd), fox_all),
                  pl.BlockSpec((None, None, t, hd), fox_all),
                  pl.BlockSpec((None, None, t // tq, tq), lambda bi, h, qi: (bi, h, 0, 0))],
        out_specs=pl.BlockSpec((None, None, tq, hd), lambda bi, h, qi: (bi, h, qi, 0)),
        out_shape=jax.ShapeDtypeStruct((b, N_HEADS_B, t, hd), F32),
        scratch_shapes=[pltpu.VMEM((t, hd), BF16)],
        compiler_params=_params("parallel", "parallel", "arbitrary"),
        name="fox_prompt",
    )(qt, kt, vt, c4)


def _rank_select(gate, n_valid, n_cand, n_sel):
    lane = lax.broadcasted_iota(jnp.int32, gate.shape, 1)
    gate = jnp.where(lane < n_valid, gate, MASKED)
    beaten = jnp.zeros(gate.shape, F32)
    for m in range(n_cand):
        gm = gate[:, m:m + 1]
        beaten = beaten + jnp.where((gm > gate) | ((gm == gate) & (m < lane)), 1.0, 0.0)
    return (beaten < n_sel) & (lane < n_valid)


def _moba_prompt_kernel(slope_ref, q_ref, k_ref, v_ref, o_ref, kb_s, km_s, m_s, l_s, acc_s, *, nb, n_sel):
    h = pl.program_id(1)
    own = pl.program_id(2)
    blk = MOBA_BLOCK

    @pl.when(own == 0)
    def _():
        km_s[...] = jnp.zeros(km_s.shape, F32)
        for n in range(nb):
            kn = k_ref[n * blk:(n + 1) * blk, :]
            kb_s[n * blk:(n + 1) * blk, :] = kn.astype(BF16)
            km_s[n:n + 1, :] = jnp.sum(kn, axis=0, keepdims=True) * (1.0 / blk)

    slope = slope_ref[h]
    q = q_ref[...]
    sel = _rank_select(_dot_nt(q, km_s[...].astype(BF16)), own, nb - 1, n_sel)
    row = lax.broadcasted_iota(jnp.int32, (blk, blk), 0)
    col = lax.broadcasted_iota(jnp.int32, (blk, blk), 1)
    pos = lax.broadcasted_iota(jnp.int32, (1, blk), 1).astype(F32)

    start = pl.multiple_of(own * blk, blk)
    s = _dot_nt(q, kb_s[pl.ds(start, blk), :]) * SCALE + slope * (pos + (own * blk).astype(F32))
    s = jnp.where(col <= row, s, MASKED)
    m = jnp.max(s, axis=-1, keepdims=True)
    p = jnp.exp(s - m)
    m_s[...] = m
    l_s[...] = jnp.sum(p, axis=-1, keepdims=True)
    acc_s[...] = _dot(p.astype(BF16), v_ref[pl.ds(start, blk), :])

    for n in range(nb - 1):
        @pl.when(n < own)
        def _(n=n):
            s = _dot_nt(q, kb_s[n * blk:(n + 1) * blk, :]) * SCALE + slope * (pos + float(n * blk))
            s = jnp.where(sel[:, n:n + 1], s, MASKED)
            m_new, l_new, acc_new = _softmax_step(s, v_ref[n * blk:(n + 1) * blk, :], m_s[...], l_s[...], acc_s[...])
            m_s[...] = m_new
            l_s[...] = l_new
            acc_s[...] = acc_new

    o_ref[...] = acc_s[...] / l_s[...]


def _moba_prompt(slopes, qt, kt, vt):
    b, _, t, hd = qt.shape
    assert t % MOBA_BLOCK == 0
    nb = t // MOBA_BLOCK
    assert nb <= LANES
    n_sel = min(MOBA_TOPK, nb - 1)
    blk = MOBA_BLOCK
    return pl.pallas_call(
        functools.partial(_moba_prompt_kernel, nb=nb, n_sel=n_sel),
        grid=(b, N_HEADS_A, nb),
        in_specs=[pl.BlockSpec(memory_space=pltpu.SMEM),
                  pl.BlockSpec((None, None, blk, hd), lambda bi, h, qi: (bi, h, qi, 0)),
                  pl.BlockSpec((None, None, t, hd), lambda bi, h, qi: (bi, h, 0, 0)),
                  pl.BlockSpec((None, None, t, hd), lambda bi, h, qi: (bi, h, 0, 0))],
        out_specs=pl.BlockSpec((None, None, blk, hd), lambda bi, h, qi: (bi, h, qi, 0)),
        out_shape=jax.ShapeDtypeStruct((b, N_HEADS_A, t, hd), F32),
        scratch_shapes=[pltpu.VMEM((t, hd), BF16), pltpu.VMEM((LANES, hd), F32),
                        pltpu.VMEM((blk, 1), F32), pltpu.VMEM((blk, 1), F32), pltpu.VMEM((blk, hd), F32)],
        compiler_params=_params("parallel", "parallel", "arbitrary"),
        name="moba_prompt",
    )(slopes, qt, kt, vt)


def _sample_kernel(pt_ref, q_ref, kn_ref, vn_ref, lfn_ref, kc_ref, vc_ref, lfc_ref, o_ref,
                   m_s, l_s, acc_s, ks_s, tp_s, *, n_pages, page, s_new, n_sel):
    del pt_ref
    p = pl.program_id(1)
    ppb = MOBA_BLOCK // page
    n_blocks = n_pages // ppb
    past = n_pages * page
    rows = q_ref.shape[0]
    lane = lax.broadcasted_iota(jnp.int32, (1, page), 1)

    @pl.when(p == 0)
    def _():
        ks_s[...] = jnp.zeros(ks_s.shape, F32)
        tp_s[...] = jnp.zeros(tp_s.shape, F32)

    def partial(h, kh, vh, bias, mask, slot):
        s = _dot_nt(q_ref[:, h * HEAD_DIM:(h + 1) * HEAD_DIM], kh) * SCALE + bias
        if mask is not None:
            s = jnp.where(mask, s, MASKED)
        m = jnp.max(s, axis=-1, keepdims=True)
        e = jnp.exp(s - m)
        m_s[h, pl.ds(slot, 1)] = jnp.broadcast_to(m, (rows, LANES))[None]
        l_s[h, pl.ds(slot, 1)] = jnp.broadcast_to(jnp.sum(e, axis=-1, keepdims=True), (rows, LANES))[None]
        acc_s[h, pl.ds(slot, 1)] = _dot(e.astype(BF16), vh)[None]

    @pl.when(p < n_pages)
    def _():
        pre = _lane_cumsum(lfc_ref[...])
        cbias = -(tp_s[...] + pre)
        tp_s[...] = tp_s[...] + pre[:, page - 1:page]
        pos = (p * page + lane).astype(F32)
        blk = p // ppb
        for h in range(N_HEADS):
            kh = kc_ref[pl.ds(h, page, stride=N_HEADS), :]
            vh = vc_ref[pl.ds(h, page, stride=N_HEADS), :].astype(BF16)
            if h < N_HEADS_A:
                bias = 2.0 ** (-8.0 * (h + 1) / N_HEADS_A) * pos
                ks_s[h, pl.ds(blk, 1), :] = ks_s[h, pl.ds(blk, 1), :] + jnp.sum(kh, axis=0, keepdims=True)
            else:
                bias = cbias[h - N_HEADS_A:h - N_HEADS_A + 1, :]
            partial(h, kh.astype(BF16), vh, bias, None, p)

    @pl.when(p == n_pages)
    def _():
        r_i = lax.broadcasted_iota(jnp.int32, (rows, LANES), 0)
        c_i = lax.broadcasted_iota(jnp.int32, (rows, LANES), 1)
        causal = (c_i <= r_i) & (c_i < s_new)
        cn = _lane_cumsum(lfn_ref[...])
        cbias = -(tp_s[...] + cn)
        pos = (past + lax.broadcasted_iota(jnp.int32, (1, LANES), 1)).astype(F32)
        outs = []
        for h in range(N_HEADS):
            hs = slice(h * HEAD_DIM, (h + 1) * HEAD_DIM)
            if h < N_HEADS_A:
                bias = 2.0 ** (-8.0 * (h + 1) / N_HEADS_A) * pos
            else:
                bias = cbias[h - N_HEADS_A:h - N_HEADS_A + 1, :]
            partial(h, kn_ref[:, hs], vn_ref[:, hs], bias, causal, n_pages)
            mh = m_s[h]
            if h < N_HEADS_A:
                kmean = (ks_s[h] * (1.0 / MOBA_BLOCK)).astype(BF16)
                sel = _rank_select(_dot_nt(q_ref[:, hs], kmean), n_blocks, n_blocks, n_sel)
                keep = [sel[:, (pp // ppb):(pp // ppb) + 1] for pp in range(n_pages)]
                m_eff = [jnp.where(keep[pp], mh[pp], MASKED) for pp in range(n_pages)] + [mh[n_pages]]
            else:
                m_eff = [mh[pp] for pp in range(n_pages + 1)]
            m_all = m_eff[0]
            for t in m_eff[1:]:
                m_all = jnp.maximum(m_all, t)
            l = jnp.zeros((rows, LANES), F32)
            acc = jnp.zeros((rows, HEAD_DIM), F32)
            for pp in range(n_pages + 1):
                w = jnp.exp(m_eff[pp] - m_all)
                l = l + w * l_s[h, pp]
                acc = acc + w[:, :HEAD_DIM] * acc_s[h, pp]
            outs.append(acc / l[:, :HEAD_DIM])
        o_ref[...] = jnp.concatenate(outs, axis=-1)


def _sample_attention(layer, page_table, s_new, q, kn, vn, lfn, cache_k2, cache_v2, cache_lft):
    bs, rows, d = q.shape
    n_pages = page_table.shape[1]
    page = cache_lft.shape[-1]
    assert page == LANES and MOBA_BLOCK % page == 0
    ppb = MOBA_BLOCK // page
    assert n_pages % ppb == 0, "past length must be a multiple of the MoBA block"
    n_blocks = n_pages // ppb
    n_sel = min(MOBA_TOPK, n_blocks)
    assert n_sel > 0 and n_blocks <= LANES

    def cache_map(b, p, pt):
        return (layer, pt[b * n_pages + jnp.minimum(p, n_pages - 1)], 0, 0)

    seq = lambda b, p, pt: (b, 0, 0)
    kernel = functools.partial(_sample_kernel, n_pages=n_pages, page=page, s_new=s_new, n_sel=n_sel)
    grid_spec = pltpu.PrefetchScalarGridSpec(
        num_scalar_prefetch=1,
        grid=(bs, n_pages + 1),
        in_specs=[pl.BlockSpec((None, rows, d), seq),
                  pl.BlockSpec((None, LANES, d), seq),
                  pl.BlockSpec((None, LANES, d), seq),
                  pl.BlockSpec((None, N_HEADS_B, LANES), seq),
                  pl.BlockSpec((None, None, page * N_HEADS, HEAD_DIM), cache_map),
                  pl.BlockSpec((None, None, page * N_HEADS, HEAD_DIM), cache_map),
                  pl.BlockSpec((None, None, N_HEADS_B, page), cache_map)],
        out_specs=pl.BlockSpec((None, rows, d), seq),
        scratch_shapes=[pltpu.VMEM((N_HEADS, n_pages + 1, rows, LANES), F32),
                        pltpu.VMEM((N_HEADS, n_pages + 1, rows, LANES), F32),
                        pltpu.VMEM((N_HEADS, n_pages + 1, rows, HEAD_DIM), F32),
                        pltpu.VMEM((N_HEADS_A, LANES, HEAD_DIM), F32),
                        pltpu.VMEM((N_HEADS_B, LANES), F32)],
    )
    return pl.pallas_call(
        kernel,
        grid_spec=grid_spec,
        out_shape=jax.ShapeDtypeStruct((bs, rows, d), F32),
        compiler_params=_params("parallel", "arbitrary"),
        name="sample_attention",
    )(page_table.reshape(-1), q, kn, vn, lfn, cache_k2, cache_v2, cache_lft)


def _merge_kernel(oa_ref, ob_ref, g_ref, x_ref, wpa_ref, wpb_ref, wo_ref, lg_ref, lb_ref, o_ref, *, alpha):
    d = x_ref.shape[1]
    ya = _dot(oa_ref[...].astype(BF16), wpa_ref[...])
    yb = _dot(ob_ref[...].astype(BF16), wpb_ref[...])
    mix = _sigmoid(g_ref[:, :d]) * ya + _sigmoid(g_ref[:, d:]) * yb
    y = _dot(mix.astype(BF16), wo_ref[...])
    o_ref[...] = _layer_norm(alpha * x_ref[...] + y, lg_ref[...], lb_ref[...])


def _merge_norm(oa, ob, g, x, wpa, wpb, wo, lg, lb, alpha):
    n, d = x.shape
    tm = _pick_tile(n, 256)
    row = lambda i: (i, 0)
    fix = lambda i: (0, 0)
    return pl.pallas_call(
        functools.partial(_merge_kernel, alpha=alpha),
        grid=(n // tm,),
        in_specs=[pl.BlockSpec((tm, oa.shape[1]), row), pl.BlockSpec((tm, ob.shape[1]), row),
                  pl.BlockSpec((tm, g.shape[1]), row), pl.BlockSpec((tm, d), row)]
                 + [pl.BlockSpec(w.shape, fix) for w in (wpa, wpb, wo, lg, lb)],
        out_specs=pl.BlockSpec((tm, d), row),
        out_shape=jax.ShapeDtypeStruct((n, d), F32),
        compiler_params=_params("parallel"),
        name="merge_norm",
    )(oa, ob, g, x, wpa, wpb, wo, lg, lb)


def _swiglu_kernel(te_ref, nv_ref, x_ref, wg_ref, wu_ref, wd_ref, lg_ref, lb_ref, o_ref, xb_s, acc_s,
                   *, nj, fuse_norm, alpha):
    del te_ref
    i = pl.program_id(0)
    j = pl.program_id(1)
    live = i < nv_ref[0]

    @pl.when(live)
    def _():
        @pl.when(j == 0)
        def _():
            xb_s[...] = x_ref[...].astype(BF16)

        xb = xb_s[...]
        hg = _dot(xb, wg_ref[...])
        hu = _dot(xb, wu_ref[...])
        part = _dot((hg * _sigmoid(hg) * hu).astype(BF16), wd_ref[...])

        @pl.when(j == 0)
        def _():
            acc_s[...] = part

        @pl.when(j > 0)
        def _():
            acc_s[...] = acc_s[...] + part

        @pl.when(j == nj - 1)
        def _():
            if fuse_norm:
                o_ref[...] = _layer_norm(alpha * x_ref[...] + acc_s[...], lg_ref[...], lb_ref[...])
            else:
                o_ref[...] = acc_s[...]

    @pl.when(jnp.logical_not(live) & (j == nj - 1))
    def _():
        o_ref[...] = jnp.zeros(o_ref.shape, F32)


def _swiglu(tile_expert, n_live, x, wg, wu, wd, lg, lb, *, tm, tf, fuse_norm, alpha):
    n, d = x.shape
    f = wg.shape[2]
    assert n % tm == 0 and f % tf == 0
    nj = f // tf
    grid_spec = pltpu.PrefetchScalarGridSpec(
        num_scalar_prefetch=2,
        grid=(n // tm, nj),
        in_specs=[pl.BlockSpec((tm, d), lambda i, j, te, nv: (i, 0)),
                  pl.BlockSpec((None, d, tf), lambda i, j, te, nv: (te[i], 0, j)),
                  pl.BlockSpec((None, d, tf), lambda i, j, te, nv: (te[i], 0, j)),
                  pl.BlockSpec((None, tf, d), lambda i, j, te, nv: (te[i], j, 0)),
                  pl.BlockSpec((1, d), lambda i, j, te, nv: (0, 0)),
                  pl.BlockSpec((1, d), lambda i, j, te, nv: (0, 0))],
        out_specs=pl.BlockSpec((tm, d), lambda i, j, te, nv: (i, 0)),
        scratch_shapes=[pltpu.VMEM((tm, d), BF16), pltpu.VMEM((tm, d), F32)],
    )
    return pl.pallas_call(
        functools.partial(_swiglu_kernel, nj=nj, fuse_norm=fuse_norm, alpha=alpha),
        grid_spec=grid_spec,
        out_shape=jax.ShapeDtypeStruct((n, d), F32),
        compiler_params=_params("parallel", "arbitrary"),
        name="swiglu_norm" if fuse_norm else "swiglu_grouped",
    )(tile_expert, n_live, x, wg, wu, wd, lg, lb)


def _router_kernel(x_ref, wr_ref, br_ref, o_ref, cnt_ref, carry_s, *, n_experts):
    i = pl.program_id(0)
    tm = x_ref.shape[0]

    @pl.when(i == 0)
    def _():
        carry_s[...] = jnp.zeros(carry_s.shape, F32)

    lane = lax.broadcasted_iota(jnp.int32, (tm, LANES), 1)
    logits = _dot(x_ref[...].astype(BF16), wr_ref[...]) + br_ref[...]
    logits = jnp.where(lane < n_experts, logits, MASKED)
    v1 = jnp.max(logits, axis=-1, keepdims=True)
    i1 = jnp.min(jnp.where(logits == v1, lane, LANES), axis=-1, keepdims=True)
    rest = jnp.where(lane == i1, MASKED, logits)
    v2 = jnp.max(rest, axis=-1, keepdims=True)
    i2 = jnp.min(jnp.where(rest == v2, lane, LANES), axis=-1, keepdims=True)
    e = jnp.exp(v2 - v1)
    g1 = 1.0 / (1.0 + e)
    g2 = e / (1.0 + e)
    oh1 = jnp.where(lane == i1, 1.0, 0.0)
    oh2 = jnp.where(lane == i2, 1.0, 0.0)
    both = oh1 + oh2
    r_i = lax.broadcasted_iota(jnp.int32, (tm, tm), 0)
    c_i = lax.broadcasted_iota(jnp.int32, (tm, tm), 1)
    earlier = jnp.where(c_i < r_i, 1.0, 0.0).astype(BF16)
    before = carry_s[...] + _dot(earlier, both.astype(BF16))
    r1 = jnp.sum(oh1 * before, axis=-1, keepdims=True)
    r2 = jnp.sum(oh2 * before, axis=-1, keepdims=True)
    carry_s[...] = carry_s[...] + jnp.sum(both, axis=0, keepdims=True)
    out = jnp.where(lane == 0, i1.astype(F32), 0.0)
    out = jnp.where(lane == 1, i2.astype(F32), out)
    out = jnp.where(lane == 2, g1, out)
    out = jnp.where(lane == 3, g2, out)
    out = jnp.where(lane == 4, r1, out)
    out = jnp.where(lane == 5, r2, out)
    o_ref[...] = out
    cnt_ref[...] = carry_s[...]


def _route(x, wr, br, n_experts):
    n, d = x.shape
    tm = _pick_tile(n, 512)
    return pl.pallas_call(
        functools.partial(_router_kernel, n_experts=n_experts),
        grid=(n // tm,),
        in_specs=[pl.BlockSpec((tm, d), lambda i: (i, 0)),
                  pl.BlockSpec(wr.shape, lambda i: (0, 0)),
                  pl.BlockSpec(br.shape, lambda i: (0, 0))],
        out_specs=[pl.BlockSpec((tm, LANES), lambda i: (i, 0)), pl.BlockSpec((1, LANES), lambda i: (0, 0))],
        out_shape=[jax.ShapeDtypeStruct((n, LANES), F32), jax.ShapeDtypeStruct((1, LANES), F32)],
        scratch_shapes=[pltpu.VMEM((1, LANES), F32)],
        compiler_params=_params("arbitrary"),
        name="moe_route",
    )(x, wr, br)


def _row_copy(src, dst, sem):
    return pltpu.make_async_copy(src, dst, sem)


def _dispatch_kernel(dest_ref, x_ref, init_ref, o_ref, sem):
    del init_ref
    ts = x_ref.shape[0]

    def issue(r, c):
        for k in range(TOP_K):
            _row_copy(x_ref.at[pl.ds(r, 1)], o_ref.at[pl.ds(dest_ref[TOP_K * r + k], 1)], sem).start()
        return c

    lax.fori_loop(0, ts, issue, 0)

    def drain(r, c):
        for k in range(TOP_K):
            _row_copy(x_ref.at[pl.ds(0, 1)], o_ref.at[pl.ds(0, 1)], sem).wait()
        return c

    lax.fori_loop(0, ts, drain, 0)


def _dispatch(dest, x, n_slots):
    n, d = x.shape
    ts = _pick_tile(n, 256)
    init = jnp.zeros((n_slots, d), F32)
    return pl.pallas_call(
        _dispatch_kernel,
        grid=(n // ts,),
        in_specs=[pl.BlockSpec((TOP_K * ts,), lambda i: (i,), memory_space=pltpu.SMEM),
                  pl.BlockSpec((ts, d), lambda i: (i, 0)),
                  pl.BlockSpec(memory_space=pl.ANY)],
        out_specs=pl.BlockSpec(memory_space=pl.ANY),
        out_shape=jax.ShapeDtypeStruct((n_slots, d), F32),
        scratch_shapes=[pltpu.SemaphoreType.DMA(())],
        input_output_aliases={2: 0},
        compiler_params=_params("arbitrary"),
        name="moe_dispatch",
    )(dest, x, init)


def _combine_kernel(dest_ref, r_ref, x_ref, y_ref, lg_ref, lb_ref, o_ref, buf, sem, *, alpha):
    ts = x_ref.shape[0]

    def issue(r, c):
        for k in range(TOP_K):
            _row_copy(y_ref.at[pl.ds(dest_ref[TOP_K * r + k], 1)], buf.at[k, pl.ds(r, 1)], sem).start()
        return c

    lax.fori_loop(0, ts, issue, 0)

    def drain(r, c):
        for k in range(TOP_K):
            _row_copy(y_ref.at[pl.ds(0, 1)], buf.at[k, pl.ds(0, 1)], sem).wait()
        return c

    lax.fori_loop(0, ts, drain, 0)
    y = r_ref[:, 2:3] * buf[0] + r_ref[:, 3:4] * buf[1]
    o_ref[...] = _layer_norm(alpha * x_ref[...] + y, lg_ref[...], lb_ref[...])


def _combine_norm(dest, routing, x, y_sorted, lg, lb, alpha):
    n, d = x.shape
    ts = _pick_tile(n, 256)
    return pl.pallas_call(
        functools.partial(_combine_kernel, alpha=alpha),
        grid=(n // ts,),
        in_specs=[pl.BlockSpec((TOP_K * ts,), lambda i: (i,), memory_space=pltpu.SMEM),
                  pl.BlockSpec((ts, LANES), lambda i: (i, 0)),
                  pl.BlockSpec((ts, d), lambda i: (i, 0)),
                  pl.BlockSpec(memory_space=pl.ANY),
                  pl.BlockSpec((1, d), lambda i: (0, 0)),
                  pl.BlockSpec((1, d), lambda i: (0, 0))],
        out_specs=pl.BlockSpec((ts, d), lambda i: (i, 0)),
        out_shape=jax.ShapeDtypeStruct((n, d), F32),
        scratch_shapes=[pltpu.VMEM((TOP_K, ts, d), F32), pltpu.SemaphoreType.DMA(())],
        compiler_params=_params("arbitrary"),
        name="moe_combine_norm",
    )(dest, routing, x, y_sorted, lg, lb)


def _moe_norm(x, wr, br, wg, wu, wd, lg, lb, alpha):
    n, d = x.shape
    n_experts = wg.shape[0]
    tm = _pick_tile(n, 512)
    routing, counts = _route(x, wr, br, n_experts)
    counts = counts[0, :n_experts].astype(jnp.int32)
    tiles = (counts + tm - 1) // tm
    tile_end = jnp.cumsum(tiles)
    group_start = (tile_end - tiles) * tm
    expert = routing[:, :TOP_K].astype(jnp.int32)
    rank = routing[:, 4:4 + TOP_K].astype(jnp.int32)
    dest = (group_start[expert] + rank).reshape(-1)
    n_tiles = (TOP_K * n) // tm + n_experts
    tile_expert = jnp.minimum(jnp.searchsorted(tile_end, jnp.arange(n_tiles), side="right"),
                              n_experts - 1).astype(jnp.int32)
    n_live = tile_end[-1:].astype(jnp.int32)
    x_sorted = _dispatch(dest, x, n_tiles * tm)
    y_sorted = _swiglu(tile_expert, n_live, x_sorted, wg, wu, wd, lg, lb,
                       tm=tm, tf=_pick_tile(wg.shape[2], 896) if wg.shape[2] % 896 == 0 else _pick_tile(wg.shape[2], 512),
                       fuse_norm=False, alpha=alpha)
    return _combine_norm(dest, routing, x, y_sorted, lg, lb, alpha)


def _dense_norm(x, wg, wu, wd, lg, lb, alpha):
    n, d = x.shape
    tm = _pick_tile(n, 512)
    f = wg.shape[2]
    tf = f // 2 if (f // 2) % LANES == 0 else _pick_tile(f, 512)
    tile_expert = jnp.zeros((n // tm,), jnp.int32)
    n_live = jnp.full((1,), n // tm, jnp.int32)
    return _swiglu(tile_expert, n_live, x, wg, wu, wd, lg, lb, tm=tm, tf=tf, fuse_norm=True, alpha=alpha)


def kernel(x_prompt, x_sample, cache_k, cache_v, cache_logf, page_table, w_in, b_f, w_pa, w_pb, w_o,
           ln1_g, ln1_b, ln2_g, ln2_b, w_gate_d, w_up_d, w_down_d, w_router, b_router,
           w_gate_e, w_up_e, w_down_e):
    depth, d_model, _ = w_in.shape
    b, t, _ = x_prompt.shape
    bs, s_new, _ = x_sample.shape
    n_p = b * t
    n_s = bs * s_new
    alpha = (2.0 * depth) ** 0.25
    n_pool, page = cache_k.shape[1], cache_k.shape[2]
    rows = SUBLANES
    assert s_new <= rows and d_model == WIDTH_A + WIDTH_B
    tq = MOBA_BLOCK

    o_qa, o_ka, o_va = 0, WIDTH_A, 2 * WIDTH_A
    o_qb = 3 * WIDTH_A
    o_kb, o_vb = o_qb + WIDTH_B, o_qb + 2 * WIDTH_B
    o_f = o_qb + 3 * WIDTH_B
    o_ga = o_f + N_HEADS_B
    o_gb = o_ga + d_model

    def cols(w, *spans):
        return jnp.concatenate([w[:, lo:lo + n] for lo, n in spans], axis=1).astype(BF16)

    slopes = 2.0 ** (-8.0 * jnp.arange(1, N_HEADS_A + 1, dtype=F32) / N_HEADS_A)
    cache_k2 = cache_k.reshape(depth, n_pool, page * N_HEADS, HEAD_DIM)
    cache_v2 = cache_v.reshape(depth, n_pool, page * N_HEADS, HEAD_DIM)
    cache_lft = jnp.swapaxes(cache_logf, 2, 3)

    x = jnp.concatenate([x_prompt.reshape(n_p, d_model), x_sample.reshape(n_s, d_model)], axis=0)
    k_out, v_out, f_out = [], [], []
    for l in range(depth):
        wl = w_in[l]
        wq = cols(wl, (o_qa, WIDTH_A), (o_qb, WIDTH_B))
        wk = cols(wl, (o_ka, WIDTH_A), (o_kb, WIDTH_B))
        wv = cols(wl, (o_va, WIDTH_A), (o_vb, WIDTH_B))
        wg = cols(wl, (o_ga, d_model), (o_gb, d_model))
        wf = jnp.pad(wl[:, o_f:o_ga], ((0, 0), (0, LANES - N_HEADS_B))).astype(BF16)
        bf = jnp.pad(b_f[l], (0, LANES - N_HEADS_B)).reshape(1, LANES)
        q, k, v, g, f = _project_in(x, wq, wk, wv, wg, wf, bf)
        logf = f[:, :N_HEADS_B]
        k_out.append(k)
        v_out.append(v)
        f_out.append(logf)

        def heads(a):
            return a[:n_p].reshape(b, t, N_HEADS, HEAD_DIM).transpose(0, 2, 1, 3)

        qt, kt, vt = heads(q), heads(k), heads(v.astype(BF16))
        c = _cumsum_rows(logf[:n_p].reshape(b, t, N_HEADS_B).transpose(0, 2, 1).reshape(b * N_HEADS_B, t))
        oa = _moba_prompt(slopes, qt, kt, vt)
        ob = _fox_prompt(qt, kt, vt, c.reshape(b, N_HEADS_B, t // tq, tq))
        oa = oa.transpose(0, 2, 1, 3).reshape(n_p, WIDTH_A)
        ob = ob.transpose(0, 2, 1, 3).reshape(n_p, WIDTH_B)

        def pad_rows(a, n):
            a = a[n_p:].reshape(bs, s_new, -1)
            return jnp.pad(a, ((0, 0), (0, n - s_new), (0, 0)))

        lfn = jnp.pad(logf[n_p:].reshape(bs, s_new, N_HEADS_B).transpose(0, 2, 1),
                      ((0, 0), (0, 0), (0, LANES - s_new)))
        o_s = _sample_attention(l, page_table, s_new, pad_rows(q, rows), pad_rows(k.astype(BF16), LANES),
                                pad_rows(v.astype(BF16), LANES), lfn, cache_k2, cache_v2, cache_lft)
        o_s = o_s[:, :s_new].reshape(n_s, d_model)
        oa = jnp.concatenate([oa, o_s[:, :WIDTH_A]], axis=0)
        ob = jnp.concatenate([ob, o_s[:, WIDTH_A:]], axis=0)

        x = _merge_norm(oa, ob, g, x, w_pa[l].astype(BF16), w_pb[l].astype(BF16), w_o[l].astype(BF16),
                        ln1_g[l].reshape(1, -1), ln1_b[l].reshape(1, -1), alpha)

        i = l // 2
        lg, lb = ln2_g[l].reshape(1, -1), ln2_b[l].reshape(1, -1)
        if l % 2 == 0:
            x = _dense_norm(x, w_gate_d[i:i + 1].astype(BF16), w_up_d[i:i + 1].astype(BF16),
                            w_down_d[i:i + 1].astype(BF16), lg, lb, alpha)
        else:
            n_experts = w_router.shape[2]
            wr = jnp.pad(w_router[i], ((0, 0), (0, LANES - n_experts))).astype(BF16)
            br = jnp.pad(b_router[i], (0, LANES - n_experts)).reshape(1, LANES)
            x = _moe_norm(x, wr, br, w_gate_e[i].astype(BF16), w_up_e[i].astype(BF16),
                          w_down_e[i].astype(BF16), lg, lb, alpha)

    def stack(parts, lo, hi, shape):
        return jnp.stack([a[lo:hi].reshape(shape) for a in parts])

    return (x[:n_p].reshape(b, t, d_model), x[n_p:].reshape(bs, s_new, d_model),
            stack(k_out, 0, n_p, (b, t, N_HEADS, HEAD_DIM)), stack(v_out, 0, n_p, (b, t, N_HEADS, HEAD_DIM)),
            stack(f_out, 0, n_p, (b, t, N_HEADS_B)),
            stack(k_out, n_p, n_p + n_s, (bs, s_new, N_HEADS, HEAD_DIM)),
            stack(v_out, n_p, n_p + n_s, (bs, s_new, N_HEADS, HEAD_DIM)),
            stack(f_out, n_p, n_p + n_s, (bs, s_new, N_HEADS_B)))
```

```python
import functools

import jax
import jax.numpy as jnp
from jax import lax
from jax.experimental import pallas as pl
from jax.experimental.pallas import tpu as pltpu

F32 = jnp.float32
BF16 = jnp.bfloat16

HEAD_DIM = 64
N_HEADS_A = 8
N_HEADS_B = 8
N_HEADS = N_HEADS_A + N_HEADS_B
WIDTH_A = N_HEADS_A * HEAD_DIM
WIDTH_B = N_HEADS_B * HEAD_DIM
MOBA_BLOCK = 256
MOBA_TOPK = 3
TOP_K = 2
LN_EPS = 1e-5
SCALE = HEAD_DIM ** -0.5
MASKED = -1e30

LANES = 128
SUBLANES = 8
PAIR = 2 * HEAD_DIM
assert PAIR == LANES
VMEM_LIMIT_BYTES = 56 * 1024 * 1024


def _params(*sem):
    return pltpu.CompilerParams(dimension_semantics=sem, vmem_limit_bytes=VMEM_LIMIT_BYTES)


def _pick_tile(n, cap):
    t = cap
    while t > SUBLANES and n % t:
        t //= 2
    assert n % t == 0, (n, cap)
    return t


def _dot(a, b):
    return jnp.dot(a, b, preferred_element_type=F32)


def _dot_nt(a, b):
    return lax.dot_general(a, b, (((1,), (1,)), ((), ())), preferred_element_type=F32)


def _sigmoid(x):
    return 1.0 / (1.0 + jnp.exp(-x))


def _layer_norm(y, g, b):
    mu = jnp.mean(y, axis=-1, keepdims=True)
    d = y - mu
    var = jnp.mean(d * d, axis=-1, keepdims=True)
    return d * lax.rsqrt(var + LN_EPS) * g + b


def _lane_cumsum(x):
    n = x.shape[-1]
    lane = lax.broadcasted_iota(jnp.int32, x.shape, x.ndim - 1)
    sh = 1
    while sh < n:
        x = x + jnp.where(lane >= sh, pltpu.roll(x, sh, x.ndim - 1), 0.0)
        sh *= 2
    return x


def _rank_select(gate, n_valid, n_cand, n_sel):
    lane = lax.broadcasted_iota(jnp.int32, gate.shape, 1)
    gate = jnp.where(lane < n_valid, gate, MASKED)
    beaten = jnp.zeros(gate.shape, F32)
    for m in range(n_cand):
        gm = gate[:, m:m + 1]
        beaten = beaten + jnp.where((gm > gate) | ((gm == gate) & (m < lane)), 1.0, 0.0)
    return (beaten < n_sel) & (lane < n_valid)


def _split_pair(q):
    lo = lax.broadcasted_iota(jnp.int32, (1, PAIR), 1) < HEAD_DIM
    zero = jnp.zeros_like(q)
    return jnp.where(lo, q, zero), jnp.where(lo, zero, q), lo


def _proj_kernel(x_ref, wq_ref, wkt_ref, wvt_ref, wg_ref, wft_ref, bft_ref,
                 q_ref, g_ref, ktp_ref, vtp_ref, ftp_ref, kts_ref, vts_ref, fts_ref, *, prompt_tiles):
    i = pl.program_id(0)
    x = x_ref[...].astype(BF16)
    q_ref[...] = _dot(x, wq_ref[...]).astype(BF16)
    g_ref[...] = _dot(x, wg_ref[...])
    kt = _dot_nt(wkt_ref[...], x)
    vt = _dot_nt(wvt_ref[...], x)
    z = _dot_nt(wft_ref[...], x)[:N_HEADS_B] + bft_ref[...]
    ft = jnp.minimum(z, 0.0) - jnp.log(1.0 + jnp.exp(-jnp.abs(z)))

    @pl.when(i < prompt_tiles)
    def _():
        ktp_ref[...] = kt
        vtp_ref[...] = vt
        ftp_ref[...] = ft

    @pl.when(i >= prompt_tiles)
    def _():
        kts_ref[...] = kt
        vts_ref[...] = vt
        fts_ref[...] = ft


def _project_in(x, wq, wkt, wvt, wg, wft, bft, b, t, n_s, tm):
    n, d = x.shape
    w = wkt.shape[0]
    assert t % tm == 0 and n_s % tm == 0
    tpb = t // tm
    prompt_tiles = b * tpb
    row = lambda i: (i, 0)
    fix = lambda i: (0, 0)

    def prompt_map(i):
        j = jnp.minimum(i, prompt_tiles - 1)
        return (j // tpb, 0, j % tpb)

    sample_map = lambda i: (0, jnp.maximum(i - prompt_tiles, 0))
    return pl.pallas_call(
        functools.partial(_proj_kernel, prompt_tiles=prompt_tiles),
        grid=(n // tm,),
        in_specs=[pl.BlockSpec((tm, d), row)] + [pl.BlockSpec(a.shape, fix) for a in (wq, wkt, wvt, wg, wft, bft)],
        out_specs=[pl.BlockSpec((tm, w), row), pl.BlockSpec((tm, wg.shape[1]), row),
                   pl.BlockSpec((None, w, tm), prompt_map), pl.BlockSpec((None, w, tm), prompt_map),
                   pl.BlockSpec((None, N_HEADS_B, tm), prompt_map),
                   pl.BlockSpec((w, tm), sample_map), pl.BlockSpec((w, tm), sample_map),
                   pl.BlockSpec((N_HEADS_B, tm), sample_map)],
        out_shape=[jax.ShapeDtypeStruct((n, w), BF16), jax.ShapeDtypeStruct((n, wg.shape[1]), F32),
                   jax.ShapeDtypeStruct((b, w, t), F32), jax.ShapeDtypeStruct((b, w, t), F32),
                   jax.ShapeDtypeStruct((b, N_HEADS_B, t), F32),
                   jax.ShapeDtypeStruct((w, n_s), F32), jax.ShapeDtypeStruct((w, n_s), F32),
                   jax.ShapeDtypeStruct((N_HEADS_B, n_s), F32)],
        compiler_params=_params("arbitrary"),
        name="project_in",
    )(x, wq, wkt, wvt, wg, wft, bft)


def _cumsum_kernel(x_ref, o_ref):
    o_ref[...] = _lane_cumsum(x_ref[...])


def _cumsum_rows(x):
    return pl.pallas_call(
        _cumsum_kernel,
        out_shape=jax.ShapeDtypeStruct(x.shape, F32),
        compiler_params=_params(),
        name="logf_cumsum",
    )(x)


def _pair_update(s0, s1, vt, lo, m0, l0, m1, l1, acc):
    m0n = jnp.maximum(m0, jnp.max(s0, axis=-1, keepdims=True))
    m1n = jnp.maximum(m1, jnp.max(s1, axis=-1, keepdims=True))
    a0 = jnp.exp(m0 - m0n)
    a1 = jnp.exp(m1 - m1n)
    p0 = jnp.exp(s0 - m0n)
    p1 = jnp.exp(s1 - m1n)
    l0 = a0 * l0 + jnp.sum(p0, axis=-1, keepdims=True)
    l1 = a1 * l1 + jnp.sum(p1, axis=-1, keepdims=True)
    pv = jnp.where(lo, _dot_nt(p0.astype(BF16), vt), _dot_nt(p1.astype(BF16), vt))
    acc = jnp.where(lo, a0, a1) * acc + pv
    return m0n, l0, m1n, l1, acc


def _fox_prompt_kernel(q_ref, kt_ref, vt_ref, c_ref, o_ref, kb_s, vb_s, *, tq):
    qi = pl.program_id(2)
    nkv = kb_s.shape[0]

    @pl.when(qi == 0)
    def _():
        for j in range(nkv):
            kb_s[j] = kt_ref[:, j * tq:(j + 1) * tq].astype(BF16)
            vb_s[j] = vt_ref[:, j * tq:(j + 1) * tq].astype(BF16)

    q0, q1, lo = _split_pair(q_ref[...])
    row = lax.broadcasted_iota(jnp.int32, (tq, tq), 0)
    col = lax.broadcasted_iota(jnp.int32, (tq, tq), 1)

    def logits(j):
        kt = kb_s[j]
        s0 = _dot(q0, kt) * SCALE - c_ref[0, pl.ds(j, 1), :]
        s1 = _dot(q1, kt) * SCALE - c_ref[1, pl.ds(j, 1), :]
        return s0, s1

    def past(j, carry):
        s0, s1 = logits(j)
        return _pair_update(s0, s1, vb_s[j], lo, *carry)

    col1 = lambda v: jnp.full((tq, 1), v, F32)
    init = (col1(MASKED), col1(0.0), col1(MASKED), col1(0.0), jnp.zeros((tq, PAIR), F32))
    carry = lax.fori_loop(0, qi, past, init)
    s0, s1 = logits(qi)
    causal = col <= row
    _, l0, _, l1, acc = _pair_update(jnp.where(causal, s0, MASKED), jnp.where(causal, s1, MASKED),
                                     vb_s[qi], lo, *carry)
    o_ref[...] = acc / jnp.where(lo, l0, l1)


def _fox_prompt(q, kt, vt, c4, b, t):
    tq = c4.shape[-1]
    nq = t // tq
    pairs = N_HEADS_B // 2
    first = N_HEADS_A // 2
    return pl.pallas_call(
        functools.partial(_fox_prompt_kernel, tq=tq),
        grid=(b, pairs, nq),
        in_specs=[pl.BlockSpec((tq, PAIR), lambda bi, p, qi: (bi * nq + qi, first + p)),
                  pl.BlockSpec((None, PAIR, t), lambda bi, p, qi: (bi, first + p, 0)),
                  pl.BlockSpec((None, PAIR, t), lambda bi, p, qi: (bi, first + p, 0)),
                  pl.BlockSpec((None, 2, nq, tq), lambda bi, p, qi: (bi, p, 0, 0))],
        out_specs=pl.BlockSpec((tq, PAIR), lambda bi, p, qi: (bi * nq + qi, p)),
        out_shape=jax.ShapeDtypeStruct((b * t, WIDTH_B), F32),
        scratch_shapes=[pltpu.VMEM((nq, PAIR, tq), BF16), pltpu.VMEM((nq, PAIR, tq), BF16)],
        compiler_params=_params("parallel", "parallel", "arbitrary"),
        name="fox_prompt",
    )(q, kt, vt, c4)


def _moba_prompt_kernel(slope_ref, q_ref, kt_ref, vt_ref, o_ref, kb_s, vb_s, km_s, m_s, l_s, acc_s, *, nb, n_sel):
    pair = pl.program_id(1)
    own = pl.program_id(2)
    blk = MOBA_BLOCK
    lane = lax.broadcasted_iota(jnp.int32, (1, LANES), 1)

    @pl.when(own == 0)
    def _():
        km = jnp.zeros((PAIR, LANES), F32)
        for n in range(nb):
            kn = kt_ref[:, n * blk:(n + 1) * blk]
            kb_s[n] = kn.astype(BF16)
            vb_s[n] = vt_ref[:, n * blk:(n + 1) * blk].astype(BF16)
            km = jnp.where(lane == n, jnp.sum(kn, axis=-1, keepdims=True) * (1.0 / blk), km)
        km_s[...] = km

    slope0 = slope_ref[2 * pair]
    slope1 = slope_ref[2 * pair + 1]
    q0, q1, lo = _split_pair(q_ref[...])
    km = km_s[...].astype(BF16)
    sel0 = _rank_select(_dot(q0, km), own, nb - 1, n_sel)
    sel1 = _rank_select(_dot(q1, km), own, nb - 1, n_sel)
    row = lax.broadcasted_iota(jnp.int32, (blk, blk), 0)
    col = lax.broadcasted_iota(jnp.int32, (blk, blk), 1)
    pos = lax.broadcasted_iota(jnp.int32, (1, blk), 1).astype(F32)

    def logits(kt, first):
        p = pos + first
        return _dot(q0, kt) * SCALE + slope0 * p, _dot(q1, kt) * SCALE + slope1 * p

    s0, s1 = logits(kb_s[own], (own * blk).astype(F32))
    causal = col <= row
    col1 = lambda v: jnp.full((blk, 1), v, F32)
    m0, l0, m1, l1, acc = _pair_update(jnp.where(causal, s0, MASKED), jnp.where(causal, s1, MASKED), vb_s[own], lo,
                                       col1(MASKED), col1(0.0), col1(MASKED), col1(0.0), jnp.zeros((blk, PAIR), F32))
    m_s[0], l_s[0], m_s[1], l_s[1] = m0, l0, m1, l1
    acc_s[...] = acc

    for n in range(nb - 1):
        @pl.when(n < own)
        def _(n=n):
            s0, s1 = logits(kb_s[n], float(n * blk))
            s0 = jnp.where(sel0[:, n:n + 1], s0, MASKED)
            s1 = jnp.where(sel1[:, n:n + 1], s1, MASKED)
            m0, l0, m1, l1, acc = _pair_update(s0, s1, vb_s[n], lo, m_s[0], l_s[0], m_s[1], l_s[1], acc_s[...])
            m_s[0], l_s[0], m_s[1], l_s[1] = m0, l0, m1, l1
            acc_s[...] = acc

    o_ref[...] = acc_s[...] / jnp.where(lo, l_s[0], l_s[1])


def _moba_prompt(slopes, q, kt, vt, b, t):
    assert t % MOBA_BLOCK == 0
    nb = t // MOBA_BLOCK
    assert nb <= LANES
    n_sel = min(MOBA_TOPK, nb - 1)
    blk = MOBA_BLOCK
    pairs = N_HEADS_A // 2
    return pl.pallas_call(
        functools.partial(_moba_prompt_kernel, nb=nb, n_sel=n_sel),
        grid=(b, pairs, nb),
        in_specs=[pl.BlockSpec(memory_space=pltpu.SMEM),
                  pl.BlockSpec((blk, PAIR), lambda bi, p, qi: (bi * nb + qi, p)),
                  pl.BlockSpec((None, PAIR, t), lambda bi, p, qi: (bi, p, 0)),
                  pl.BlockSpec((None, PAIR, t), lambda bi, p, qi: (bi, p, 0))],
        out_specs=pl.BlockSpec((blk, PAIR), lambda bi, p, qi: (bi * nb + qi, p)),
        out_shape=jax.ShapeDtypeStruct((b * t, WIDTH_A), F32),
        scratch_shapes=[pltpu.VMEM((nb, PAIR, blk), BF16), pltpu.VMEM((nb, PAIR, blk), BF16),
                        pltpu.VMEM((PAIR, LANES), F32),
                        pltpu.VMEM((2, blk, 1), F32), pltpu.VMEM((2, blk, 1), F32), pltpu.VMEM((blk, PAIR), F32)],
        compiler_params=_params("parallel", "parallel", "arbitrary"),
        name="moba_prompt",
    )(slopes, q, kt, vt)


def _sample_kernel(pt_ref, q_ref, knt_ref, vnt_ref, lfn_ref, slope_ref, *rest, n_blocks, ppb, s_new, n_sel):
    del pt_ref
    kc, vc, lfc = rest[:ppb], rest[ppb:2 * ppb], rest[2 * ppb:3 * ppb]
    o_ref, qbd_s, m_s, l_s, acc_s, km_s, tp_s = rest[3 * ppb:]
    p = pl.program_id(1)
    rows = q_ref.shape[0]
    half = N_HEADS_A * rows
    page = LANES
    lane = lax.broadcasted_iota(jnp.int32, (1, LANES), 1)
    lo = lane < HEAD_DIM

    @pl.when(p == 0)
    def _():
        q = q_ref[...].astype(F32)
        col_head = jnp.right_shift(lax.broadcasted_iota(jnp.int32, (1, q.shape[1]), 1), HEAD_DIM.bit_length() - 1)
        qbd_s[...] = jnp.concatenate([jnp.where(col_head == h, q, 0.0) for h in range(N_HEADS)], axis=0).astype(BF16)
        m_s[...] = jnp.full(m_s.shape, MASKED, F32)
        l_s[...] = jnp.zeros(l_s.shape, F32)
        km_s[...] = jnp.zeros(km_s.shape, F32)
        tp_s[...] = jnp.zeros(tp_s.shape, F32)

    def head_rows(x):
        return jnp.concatenate([jnp.broadcast_to(x[h:h + 1], (rows, x.shape[1])) for h in range(x.shape[0])], axis=0)

    def pair_rows(wide):
        return jnp.concatenate([jnp.where(lo, wide[2 * rows * j:2 * rows * j + rows],
                                          wide[2 * rows * j + rows:2 * rows * (j + 1)])
                                for j in range(N_HEADS // 2)], axis=0)

    def block_partial(kts, vts, bias, mask, slot):
        qbd = qbd_s[...]
        s = jnp.concatenate([_dot(qbd, kt) for kt in kts], axis=1) * SCALE + bias
        if mask is not None:
            s = jnp.where(mask, s, MASKED)
        m = jnp.max(s, axis=-1, keepdims=True)
        e = jnp.exp(s - m)
        l = jnp.sum(e, axis=-1, keepdims=True)
        eb = e.astype(BF16)
        outs = []
        for j in range(N_HEADS // 2):
            r = None
            for a, vt in enumerate(vts):
                term = _dot_nt(eb[2 * rows * j:2 * rows * (j + 1), a * LANES:(a + 1) * LANES],
                               vt[PAIR * j:PAIR * (j + 1), :])
                r = term if r is None else r + term
            outs.append(jnp.where(lo, r[:rows], r[rows:]))
        acc_s[pl.ds(slot, 1)] = jnp.concatenate(outs, axis=0)[None]
        m_s[...] = jnp.where(lane == slot, m, m_s[...])
        l_s[...] = jnp.where(lane == slot, l, l_s[...])

    @pl.when(p < n_blocks)
    def _():
        base = tp_s[...]
        cbs, kts, vts = [], [], []
        ksum = None
        for a in range(ppb):
            pre = _lane_cumsum(lfc[a][...])
            cbs.append(-(base + pre))
            base = base + pre[:, page - 1:page]
            kf = kc[a][...]
            kts.append(kf.astype(BF16))
            vts.append(vc[a][...].astype(BF16))
            rs = jnp.sum(kf[:WIDTH_A], axis=-1, keepdims=True)
            ksum = rs if ksum is None else ksum + rs
        tp_s[...] = base
        km_s[...] = km_s[...] + ksum * jnp.where(lane == p, 1.0, 0.0)
        pos = (p * MOBA_BLOCK + lax.broadcasted_iota(jnp.int32, (1, ppb * page), 1)).astype(F32)
        bias = jnp.concatenate([slope_ref[:half, :1] * pos, head_rows(jnp.concatenate(cbs, axis=1))], axis=0)
        block_partial(kts, vts, bias, None, p)

    @pl.when(p == n_blocks)
    def _():
        cb = -(tp_s[...] + _lane_cumsum(lfn_ref[...]))
        pos = (n_blocks * MOBA_BLOCK + lane).astype(F32)
        bias = jnp.concatenate([slope_ref[:half, :1] * pos, head_rows(cb)], axis=0)
        nrow = N_HEADS * rows
        r_i = jnp.bitwise_and(lax.broadcasted_iota(jnp.int32, (nrow, LANES), 0), rows - 1)
        c_i = lax.broadcasted_iota(jnp.int32, (nrow, LANES), 1)
        block_partial([knt_ref[...]], [vnt_ref[...]], bias, (c_i <= r_i) & (c_i < s_new), n_blocks)

        kmean = (km_s[...] * (1.0 / MOBA_BLOCK)).astype(BF16)
        gate = _dot(qbd_s[:half, :WIDTH_A], kmean)
        sel = jnp.where(_rank_select(gate, n_blocks, n_blocks, n_sel) | (lane == n_blocks), 1.0, 0.0)
        fox = jnp.broadcast_to(jnp.where(lane <= n_blocks, 1.0, 0.0), (nrow - half, LANES))
        m_eff = jnp.where(jnp.concatenate([sel, fox], axis=0) > 0.5, m_s[...], MASKED)
        w = jnp.exp(m_eff - jnp.max(m_eff, axis=-1, keepdims=True))
        l = jnp.sum(w * l_s[...], axis=-1, keepdims=True)
        acc = jnp.zeros(acc_s.shape[1:], F32)
        for slot in range(n_blocks + 1):
            acc = acc + pair_rows(jnp.broadcast_to(w[:, slot:slot + 1], (nrow, LANES))) * acc_s[slot]
        out = acc / pair_rows(jnp.broadcast_to(l, (nrow, LANES)))
        o_ref[...] = jnp.concatenate([out[rows * j:rows * (j + 1)] for j in range(N_HEADS // 2)], axis=1)


def _sample_attention(layer, page_table, s_new, q, knt, vnt, lfn, slope_rows, cache_kt, cache_vt, cache_lft):
    bs, rows, d = q.shape
    n_pages = page_table.shape[1]
    page = cache_kt.shape[-1]
    assert page == LANES and MOBA_BLOCK % page == 0 and rows == SUBLANES
    ppb = MOBA_BLOCK // page
    assert n_pages % ppb == 0, "past length must be a multiple of the MoBA block"
    n_blocks = n_pages // ppb
    n_sel = min(MOBA_TOPK, n_blocks)
    assert 0 < n_blocks < LANES

    def cache_map(a):
        return lambda b, p, pt: (layer, pt[b * n_pages + jnp.minimum(p, n_blocks - 1) * ppb + a], 0, 0)

    seq = lambda b, p, pt: (b, 0, 0)
    nrow = N_HEADS * rows
    grid_spec = pltpu.PrefetchScalarGridSpec(
        num_scalar_prefetch=1,
        grid=(bs, n_blocks + 1),
        in_specs=[pl.BlockSpec((None, rows, d), seq),
                  pl.BlockSpec((None, d, LANES), seq),
                  pl.BlockSpec((None, d, LANES), seq),
                  pl.BlockSpec((None, N_HEADS_B, LANES), seq),
                  pl.BlockSpec((nrow, LANES), lambda b, p, pt: (0, 0))]
                 + [pl.BlockSpec((None, None, d, page), cache_map(a)) for a in range(ppb)]
                 + [pl.BlockSpec((None, None, d, page), cache_map(a)) for a in range(ppb)]
                 + [pl.BlockSpec((None, None, N_HEADS_B, page), cache_map(a)) for a in range(ppb)],
        out_specs=pl.BlockSpec((None, rows, d), seq),
        scratch_shapes=[pltpu.VMEM((nrow, d), BF16),
                        pltpu.VMEM((nrow, LANES), F32), pltpu.VMEM((nrow, LANES), F32),
                        pltpu.VMEM((n_blocks + 1, nrow // 2, LANES), F32),
                        pltpu.VMEM((WIDTH_A, LANES), F32), pltpu.VMEM((N_HEADS_B, LANES), F32)],
    )
    return pl.pallas_call(
        functools.partial(_sample_kernel, n_blocks=n_blocks, ppb=ppb, s_new=s_new, n_sel=n_sel),
        grid_spec=grid_spec,
        out_shape=jax.ShapeDtypeStruct((bs, rows, d), F32),
        compiler_params=_params("parallel", "arbitrary"),
        name="sample_attention",
    )(page_table.reshape(-1), q, knt, vnt, lfn, slope_rows,
      *([cache_kt] * ppb), *([cache_vt] * ppb), *([cache_lft] * ppb))


def _merge_kernel(oa_ref, ob_ref, g_ref, x_ref, wpa_ref, wpb_ref, wo_ref, lg_ref, lb_ref, o_ref, *, alpha):
    d = x_ref.shape[1]
    ya = _dot(oa_ref[...].astype(BF16), wpa_ref[...])
    yb = _dot(ob_ref[...].astype(BF16), wpb_ref[...])
    mix = _sigmoid(g_ref[:, :d]) * ya + _sigmoid(g_ref[:, d:]) * yb
    y = _dot(mix.astype(BF16), wo_ref[...])
    o_ref[...] = _layer_norm(alpha * x_ref[...] + y, lg_ref[...], lb_ref[...])


def _merge_norm(oa, ob, g, x, wpa, wpb, wo, lg, lb, alpha):
    n, d = x.shape
    tm = _pick_tile(n, 256)
    row = lambda i: (i, 0)
    fix = lambda i: (0, 0)
    return pl.pallas_call(
        functools.partial(_merge_kernel, alpha=alpha),
        grid=(n // tm,),
        in_specs=[pl.BlockSpec((tm, oa.shape[1]), row), pl.BlockSpec((tm, ob.shape[1]), row),
                  pl.BlockSpec((tm, g.shape[1]), row), pl.BlockSpec((tm, d), row)]
                 + [pl.BlockSpec(w.shape, fix) for w in (wpa, wpb, wo, lg, lb)],
        out_specs=pl.BlockSpec((tm, d), row),
        out_shape=jax.ShapeDtypeStruct((n, d), F32),
        compiler_params=_params("parallel"),
        name="merge_norm",
    )(oa, ob, g, x, wpa, wpb, wo, lg, lb)


def _swiglu_kernel(te_ref, nv_ref, x_ref, wg_ref, wu_ref, wd_ref, lg_ref, lb_ref, o_ref, xb_s, acc_s,
                   *, nj, fuse_norm, alpha):
    del te_ref
    i = pl.program_id(0)
    j = pl.program_id(1)
    live = i < nv_ref[0]

    @pl.when(live)
    def _():
        @pl.when(j == 0)
        def _():
            xb_s[...] = x_ref[...].astype(BF16)

        xb = xb_s[...]
        hg = _dot(xb, wg_ref[...])
        hu = _dot(xb, wu_ref[...])
        part = _dot((hg * _sigmoid(hg) * hu).astype(BF16), wd_ref[...])

        @pl.when(j == 0)
        def _():
            acc_s[...] = part

        @pl.when(j > 0)
        def _():
            acc_s[...] = acc_s[...] + part

        @pl.when(j == nj - 1)
        def _():
            if fuse_norm:
                o_ref[...] = _layer_norm(alpha * x_ref[...] + acc_s[...], lg_ref[...], lb_ref[...])
            else:
                o_ref[...] = acc_s[...]

    @pl.when(jnp.logical_not(live) & (j == nj - 1))
    def _():
        o_ref[...] = jnp.zeros(o_ref.shape, F32)


def _swiglu(tile_expert, n_live, x, wg, wu, wd, lg, lb, *, tm, tf, fuse_norm, alpha):
    n, d = x.shape
    f = wg.shape[2]
    assert n % tm == 0 and f % tf == 0
    nj = f // tf
    grid_spec = pltpu.PrefetchScalarGridSpec(
        num_scalar_prefetch=2,
        grid=(n // tm, nj),
        in_specs=[pl.BlockSpec((tm, d), lambda i, j, te, nv: (i, 0)),
                  pl.BlockSpec((None, d, tf), lambda i, j, te, nv: (te[i], 0, j)),
                  pl.BlockSpec((None, d, tf), lambda i, j, te, nv: (te[i], 0, j)),
                  pl.BlockSpec((None, tf, d), lambda i, j, te, nv: (te[i], j, 0)),
                  pl.BlockSpec((1, d), lambda i, j, te, nv: (0, 0)),
                  pl.BlockSpec((1, d), lambda i, j, te, nv: (0, 0))],
        out_specs=pl.BlockSpec((tm, d), lambda i, j, te, nv: (i, 0)),
        scratch_shapes=[pltpu.VMEM((tm, d), BF16), pltpu.VMEM((tm, d), F32)],
    )
    return pl.pallas_call(
        functools.partial(_swiglu_kernel, nj=nj, fuse_norm=fuse_norm, alpha=alpha),
        grid_spec=grid_spec,
        out_shape=jax.ShapeDtypeStruct((n, d), F32),
        compiler_params=_params("parallel", "arbitrary"),
        name="swiglu_norm" if fuse_norm else "swiglu_grouped",
    )(tile_expert, n_live, x, wg, wu, wd, lg, lb)


def _hidden_tile(f, cap):
    best = None
    for tf in range(LANES, min(f, cap) + 1, LANES):
        if f % tf == 0:
            best = tf
    assert best is not None, f
    return best


def _router_kernel(x_ref, wr_ref, br_ref, o_ref, cnt_ref, carry_s, *, n_experts):
    i = pl.program_id(0)
    tm = x_ref.shape[0]

    @pl.when(i == 0)
    def _():
        carry_s[...] = jnp.zeros(carry_s.shape, F32)

    lane = lax.broadcasted_iota(jnp.int32, (tm, LANES), 1)
    logits = _dot(x_ref[...].astype(BF16), wr_ref[...]) + br_ref[...]
    logits = jnp.where(lane < n_experts, logits, MASKED)
    v1 = jnp.max(logits, axis=-1, keepdims=True)
    i1 = jnp.min(jnp.where(logits == v1, lane, LANES), axis=-1, keepdims=True)
    rest = jnp.where(lane == i1, MASKED, logits)
    v2 = jnp.max(rest, axis=-1, keepdims=True)
    i2 = jnp.min(jnp.where(rest == v2, lane, LANES), axis=-1, keepdims=True)
    e = jnp.exp(v2 - v1)
    g1 = 1.0 / (1.0 + e)
    g2 = e / (1.0 + e)
    oh1 = jnp.where(lane == i1, 1.0, 0.0)
    oh2 = jnp.where(lane == i2, 1.0, 0.0)
    both = oh1 + oh2
    r_i = lax.broadcasted_iota(jnp.int32, (tm, tm), 0)
    c_i = lax.broadcasted_iota(jnp.int32, (tm, tm), 1)
    earlier = jnp.where(c_i < r_i, 1.0, 0.0).astype(BF16)
    before = carry_s[...] + _dot(earlier, both.astype(BF16))
    r1 = jnp.sum(oh1 * before, axis=-1, keepdims=True)
    r2 = jnp.sum(oh2 * before, axis=-1, keepdims=True)
    carry_s[...] = carry_s[...] + jnp.sum(both, axis=0, keepdims=True)
    out = jnp.where(lane == 0, i1.astype(F32), 0.0)
    out = jnp.where(lane == 1, i2.astype(F32), out)
    out = jnp.where(lane == 2, g1, out)
    out = jnp.where(lane == 3, g2, out)
    out = jnp.where(lane == 4, r1, out)
    out = jnp.where(lane == 5, r2, out)
    o_ref[...] = out
    cnt_ref[...] = carry_s[...]


def _route(x, wr, br, n_experts):
    n, d = x.shape
    tm = _pick_tile(n, 512)
    return pl.pallas_call(
        functools.partial(_router_kernel, n_experts=n_experts),
        grid=(n // tm,),
        in_specs=[pl.BlockSpec((tm, d), lambda i: (i, 0)),
                  pl.BlockSpec(wr.shape, lambda i: (0, 0)),
                  pl.BlockSpec(br.shape, lambda i: (0, 0))],
        out_specs=[pl.BlockSpec((tm, LANES), lambda i: (i, 0)), pl.BlockSpec((1, LANES), lambda i: (0, 0))],
        out_shape=[jax.ShapeDtypeStruct((n, LANES), F32), jax.ShapeDtypeStruct((1, LANES), F32)],
        scratch_shapes=[pltpu.VMEM((1, LANES), F32)],
        compiler_params=_params("arbitrary"),
        name="moe_route",
    )(x, wr, br)


def _row_copy(src, dst, sem):
    return pltpu.make_async_copy(src, dst, sem)


def _dispatch_kernel(dest_ref, x_ref, init_ref, o_ref, sem):
    del init_ref
    ts = x_ref.shape[0]

    def issue(r, c):
        for k in range(TOP_K):
            _row_copy(x_ref.at[pl.ds(r, 1)], o_ref.at[pl.ds(dest_ref[TOP_K * r + k], 1)], sem).start()
        return c

    lax.fori_loop(0, ts, issue, 0)

    def drain(r, c):
        for k in range(TOP_K):
            _row_copy(x_ref.at[pl.ds(0, 1)], o_ref.at[pl.ds(0, 1)], sem).wait()
        return c

    lax.fori_loop(0, ts, drain, 0)


def _dispatch(dest, x, n_slots):
    n, d = x.shape
    ts = _pick_tile(n, 256)
    init = jnp.zeros((n_slots, d), F32)
    return pl.pallas_call(
        _dispatch_kernel,
        grid=(n // ts,),
        in_specs=[pl.BlockSpec((TOP_K * ts,), lambda i: (i,), memory_space=pltpu.SMEM),
                  pl.BlockSpec((ts, d), lambda i: (i, 0)),
                  pl.BlockSpec(memory_space=pl.ANY)],
        out_specs=pl.BlockSpec(memory_space=pl.ANY),
        out_shape=jax.ShapeDtypeStruct((n_slots, d), F32),
        scratch_shapes=[pltpu.SemaphoreType.DMA(())],
        input_output_aliases={2: 0},
        compiler_params=_params("arbitrary"),
        name="moe_dispatch",
    )(dest, x, init)


def _combine_kernel(dest_ref, r_ref, x_ref, y_ref, lg_ref, lb_ref, o_ref, buf, sem, *, alpha):
    ts = x_ref.shape[0]

    def issue(r, c):
        for k in range(TOP_K):
            _row_copy(y_ref.at[pl.ds(dest_ref[TOP_K * r + k], 1)], buf.at[k, pl.ds(r, 1)], sem).start()
        return c

    lax.fori_loop(0, ts, issue, 0)

    def drain(r, c):
        for k in range(TOP_K):
            _row_copy(y_ref.at[pl.ds(0, 1)], buf.at[k, pl.ds(0, 1)], sem).wait()
        return c

    lax.fori_loop(0, ts, drain, 0)
    y = r_ref[:, 2:3] * buf[0] + r_ref[:, 3:4] * buf[1]
    o_ref[...] = _layer_norm(alpha * x_ref[...] + y, lg_ref[...], lb_ref[...])


def _combine_norm(dest, routing, x, y_sorted, lg, lb, alpha):
    n, d = x.shape
    ts = _pick_tile(n, 256)
    return pl.pallas_call(
        functools.partial(_combine_kernel, alpha=alpha),
        grid=(n // ts,),
        in_specs=[pl.BlockSpec((TOP_K * ts,), lambda i: (i,), memory_space=pltpu.SMEM),
                  pl.BlockSpec((ts, LANES), lambda i: (i, 0)),
                  pl.BlockSpec((ts, d), lambda i: (i, 0)),
                  pl.BlockSpec(memory_space=pl.ANY),
                  pl.BlockSpec((1, d), lambda i: (0, 0)),
                  pl.BlockSpec((1, d), lambda i: (0, 0))],
        out_specs=pl.BlockSpec((ts, d), lambda i: (i, 0)),
        out_shape=jax.ShapeDtypeStruct((n, d), F32),
        scratch_shapes=[pltpu.VMEM((TOP_K, ts, d), F32), pltpu.SemaphoreType.DMA(())],
        compiler_params=_params("arbitrary"),
        name="moe_combine_norm",
    )(dest, routing, x, y_sorted, lg, lb)


def _moe_norm(x, wr, br, wg, wu, wd, lg, lb, alpha):
    n, d = x.shape
    n_experts = wg.shape[0]
    tm = _pick_tile(n, 512)
    routing, counts = _route(x, wr, br, n_experts)
    counts = counts[0, :n_experts].astype(jnp.int32)
    tiles = (counts + tm - 1) // tm
    tile_end = jnp.cumsum(tiles)
    group_start = (tile_end - tiles) * tm
    expert = routing[:, :TOP_K].astype(jnp.int32)
    rank = routing[:, 4:4 + TOP_K].astype(jnp.int32)
    dest = (group_start[expert] + rank).reshape(-1)
    n_tiles = (TOP_K * n) // tm + n_experts
    ended = (tile_end[None, :] <= jnp.arange(n_tiles)[:, None]).astype(jnp.int32)
    tile_expert = jnp.minimum(jnp.sum(ended, axis=1), n_experts - 1).astype(jnp.int32)
    n_live = tile_end[-1:].astype(jnp.int32)
    x_sorted = _dispatch(dest, x, n_tiles * tm)
    y_sorted = _swiglu(tile_expert, n_live, x_sorted, wg, wu, wd, lg, lb,
                       tm=tm, tf=_hidden_tile(wg.shape[2], 1024), fuse_norm=False, alpha=alpha)
    return _combine_norm(dest, routing, x, y_sorted, lg, lb, alpha)


def _dense_norm(x, wg, wu, wd, lg, lb, alpha):
    n, d = x.shape
    tm = _pick_tile(n, 512)
    tile_expert = jnp.zeros((n // tm,), jnp.int32)
    n_live = jnp.full((1,), n // tm, jnp.int32)
    return _swiglu(tile_expert, n_live, x, wg, wu, wd, lg, lb,
                   tm=tm, tf=_hidden_tile(wg.shape[2], 1536), fuse_norm=True, alpha=alpha)


def kernel(x_prompt, x_sample, cache_k, cache_v, cache_logf, page_table, w_in, b_f, w_pa, w_pb, w_o,
           ln1_g, ln1_b, ln2_g, ln2_b, w_gate_d, w_up_d, w_down_d, w_router, b_router,
           w_gate_e, w_up_e, w_down_e):
    depth, d_model, _ = w_in.shape
    b, t, _ = x_prompt.shape
    bs, s_new, _ = x_sample.shape
    n_p = b * t
    n_s = bs * s_new
    alpha = (2.0 * depth) ** 0.25
    n_pool, page = cache_k.shape[1], cache_k.shape[2]
    rows = SUBLANES
    width = WIDTH_A + WIDTH_B
    assert s_new <= rows and d_model == width
    tq = MOBA_BLOCK
    tm = 256 if n_s % 256 == 0 else LANES

    o_qa, o_ka, o_va = 0, WIDTH_A, 2 * WIDTH_A
    o_qb = 3 * WIDTH_A
    o_kb, o_vb = o_qb + WIDTH_B, o_qb + 2 * WIDTH_B
    o_f = o_qb + 3 * WIDTH_B
    o_ga = o_f + N_HEADS_B
    o_gb = o_ga + d_model

    def cols(w, *spans):
        return jnp.concatenate([w[:, lo:lo + n] for lo, n in spans], axis=1).astype(BF16)

    slopes = 2.0 ** (-8.0 * jnp.arange(1, N_HEADS_A + 1, dtype=F32) / N_HEADS_A)
    slope_rows = jnp.broadcast_to(
        jnp.repeat(jnp.concatenate([slopes, jnp.zeros((N_HEADS_B,), F32)]), rows)[:, None], (N_HEADS * rows, LANES))
    cache_kt = jnp.transpose(cache_k, (0, 1, 3, 4, 2)).reshape(depth, n_pool, width, page)
    cache_vt = jnp.transpose(cache_v, (0, 1, 3, 4, 2)).reshape(depth, n_pool, width, page)
    cache_lft = jnp.swapaxes(cache_logf, 2, 3)

    x = jnp.concatenate([x_prompt.reshape(n_p, d_model), x_sample.reshape(n_s, d_model)], axis=0)
    ktp, vtp, ftp, kts, vts, fts = [], [], [], [], [], []
    for l in range(depth):
        wl = w_in[l]
        wq = cols(wl, (o_qa, WIDTH_A), (o_qb, WIDTH_B))
        wkt = cols(wl, (o_ka, WIDTH_A), (o_kb, WIDTH_B)).T
        wvt = cols(wl, (o_va, WIDTH_A), (o_vb, WIDTH_B)).T
        wg = cols(wl, (o_ga, d_model), (o_gb, d_model))
        wft = jnp.pad(wl[:, o_f:o_ga].T, ((0, 2 * SUBLANES - N_HEADS_B), (0, 0))).astype(BF16)
        q, g, kt_p, vt_p, ft_p, kt_s, vt_s, ft_s = _project_in(
            x, wq, wkt, wvt, wg, wft, b_f[l].reshape(N_HEADS_B, 1), b, t, n_s, tm)
        for acc, a in zip((ktp, vtp, ftp, kts, vts, fts), (kt_p, vt_p, ft_p, kt_s, vt_s, ft_s)):
            acc.append(a)

        c = _cumsum_rows(ft_p.reshape(b * N_HEADS_B, t)).reshape(b, N_HEADS_B, t // tq, tq)
        oa = _moba_prompt(slopes, q, kt_p, vt_p, b, t)
        ob = _fox_prompt(q, kt_p, vt_p, c, b, t)

        def new_tokens(a, dtype):
            a = a.reshape(a.shape[0], bs, s_new).transpose(1, 0, 2).astype(dtype)
            return jnp.pad(a, ((0, 0), (0, 0), (0, LANES - s_new)))

        q_s = jnp.pad(q[n_p:].reshape(bs, s_new, width), ((0, 0), (0, rows - s_new), (0, 0)))
        o_s = _sample_attention(l, page_table, s_new, q_s, new_tokens(kt_s, BF16), new_tokens(vt_s, BF16),
                                new_tokens(ft_s, F32), slope_rows, cache_kt, cache_vt, cache_lft)
        o_s = o_s[:, :s_new].reshape(n_s, width)
        oa = jnp.concatenate([oa, o_s[:, :WIDTH_A]], axis=0)
        ob = jnp.concatenate([ob, o_s[:, WIDTH_A:]], axis=0)

        x = _merge_norm(oa, ob, g, x, w_pa[l].astype(BF16), w_pb[l].astype(BF16), w_o[l].astype(BF16),
                        ln1_g[l].reshape(1, -1), ln1_b[l].reshape(1, -1), alpha)

        i = l // 2
        lg, lb = ln2_g[l].reshape(1, -1), ln2_b[l].reshape(1, -1)
        if l % 2 == 0:
            x = _dense_norm(x, w_gate_d[i:i + 1].astype(BF16), w_up_d[i:i + 1].astype(BF16),
                            w_down_d[i:i + 1].astype(BF16), lg, lb, alpha)
        else:
            n_experts = w_router.shape[2]
            wr = jnp.pad(w_router[i], ((0, 0), (0, LANES - n_experts))).astype(BF16)
            br = jnp.pad(b_router[i], (0, LANES - n_experts)).reshape(1, LANES)
            x = _moe_norm(x, wr, br, w_gate_e[i].astype(BF16), w_up_e[i].astype(BF16),
                          w_down_e[i].astype(BF16), lg, lb, alpha)

    def prompt_heads(parts, heads):
        a = jnp.stack(parts).reshape(depth, b, heads, -1, t)
        return jnp.transpose(a, (0, 1, 4, 2, 3))

    def sample_heads(parts, heads):
        a = jnp.stack(parts).reshape(depth, heads, -1, bs, s_new)
        return jnp.transpose(a, (0, 3, 4, 1, 2))

    return (x[:n_p].reshape(b, t, d_model), x[n_p:].reshape(bs, s_new, d_model),
            prompt_heads(ktp, N_HEADS), prompt_heads(vtp, N_HEADS),
            prompt_heads(ftp, N_HEADS_B).reshape(depth, b, t, N_HEADS_B),
            sample_heads(kts, N_HEADS), sample_heads(vts, N_HEADS),
            sample_heads(fts, N_HEADS_B).reshape(depth, bs, s_new, N_HEADS_B))
```

```python
import functools

import jax
import jax.numpy as jnp
from jax import lax
from jax.experimental import pallas as pl
from jax.experimental.pallas import tpu as pltpu

F32 = jnp.float32
BF16 = jnp.bfloat16

HEAD_DIM = 64
N_HEADS_A = 8
N_HEADS_B = 8
N_HEADS = N_HEADS_A + N_HEADS_B
WIDTH_A = N_HEADS_A * HEAD_DIM
WIDTH_B = N_HEADS_B * HEAD_DIM
MOBA_BLOCK = 256
MOBA_TOPK = 3
TOP_K = 2
LN_EPS = 1e-5
SCALE = HEAD_DIM ** -0.5
MASKED = -1e30

LANES = 128
SUBLANES = 8
PAIR = 2 * HEAD_DIM
assert PAIR == LANES
VMEM_LIMIT_BYTES = 56 * 1024 * 1024


def _params(*sem):
    return pltpu.CompilerParams(dimension_semantics=sem, vmem_limit_bytes=VMEM_LIMIT_BYTES)


def _pick_tile(n, cap):
    t = cap
    while t > SUBLANES and n % t:
        t //= 2
    assert n % t == 0, (n, cap)
    return t


def _dot(a, b):
    return jnp.dot(a, b, preferred_element_type=F32)


def _dot_nt(a, b):
    return lax.dot_general(a, b, (((1,), (1,)), ((), ())), preferred_element_type=F32)


def _sigmoid(x):
    return 1.0 / (1.0 + jnp.exp(-x))


def _layer_norm(y, g, b):
    mu = jnp.mean(y, axis=-1, keepdims=True)
    d = y - mu
    var = jnp.mean(d * d, axis=-1, keepdims=True)
    return d * lax.rsqrt(var + LN_EPS) * g + b


def _lane_cumsum(x):
    n = x.shape[-1]
    lane = lax.broadcasted_iota(jnp.int32, x.shape, x.ndim - 1)
    sh = 1
    while sh < n:
        x = x + jnp.where(lane >= sh, pltpu.roll(x, sh, x.ndim - 1), 0.0)
        sh *= 2
    return x


def _rank_select(gate, n_valid, n_cand, n_sel):
    lane = lax.broadcasted_iota(jnp.int32, gate.shape, 1)
    gate = jnp.where(lane < n_valid, gate, MASKED)
    beaten = jnp.zeros(gate.shape, F32)
    for m in range(n_cand):
        gm = gate[:, m:m + 1]
        beaten = beaten + jnp.where((gm > gate) | ((gm == gate) & (m < lane)), 1.0, 0.0)
    return (beaten < n_sel) & (lane < n_valid)


def _rank_select_rows(gate, n_valid, n_cand, n_sel):
    row = lax.broadcasted_iota(jnp.int32, gate.shape, 0)
    gate = jnp.where(row < n_valid, gate, MASKED)
    beaten = jnp.zeros(gate.shape, F32)
    for m in range(n_cand):
        gm = gate[m:m + 1, :]
        beaten = beaten + jnp.where((gm > gate) | ((gm == gate) & (m < row)), 1.0, 0.0)
    sel = jnp.where((beaten < n_sel) & (row < n_valid), 1.0, 0.0)
    pad = jnp.zeros((LANES - gate.shape[0], gate.shape[1]), F32)
    return jnp.concatenate([sel, pad], axis=0).T


def _split_pair(q):
    lo = lax.broadcasted_iota(jnp.int32, (1, PAIR), 1) < HEAD_DIM
    zero = jnp.zeros_like(q)
    return jnp.where(lo, q, zero), jnp.where(lo, zero, q), lo


def _proj_kernel(x_ref, wq_ref, wkt_ref, wvt_ref, wg_ref, wft_ref, bft_ref,
                 q_ref, g_ref, ktp_ref, vtp_ref, ftp_ref, kts_ref, vts_ref, fts_ref, *, prompt_tiles):
    i = pl.program_id(0)
    x = x_ref[...].astype(BF16)
    q_ref[...] = _dot(x, wq_ref[...]).astype(BF16)
    g_ref[...] = _dot(x, wg_ref[...])
    kt = _dot_nt(wkt_ref[...], x)
    vt = _dot_nt(wvt_ref[...], x)
    z = _dot_nt(wft_ref[...], x)[:N_HEADS_B] + bft_ref[...]
    ft = jnp.minimum(z, 0.0) - jnp.log(1.0 + jnp.exp(-jnp.abs(z)))

    @pl.when(i < prompt_tiles)
    def _():
        ktp_ref[...] = kt
        vtp_ref[...] = vt
        ftp_ref[...] = ft

    @pl.when(i >= prompt_tiles)
    def _():
        kts_ref[...] = kt
        vts_ref[...] = vt
        fts_ref[...] = ft


def _project_in(x, wq, wkt, wvt, wg, wft, bft, b, t, n_s, tm):
    n, d = x.shape
    w = wkt.shape[0]
    assert t % tm == 0 and n_s % tm == 0
    tpb = t // tm
    prompt_tiles = b * tpb
    row = lambda i: (i, 0)
    fix = lambda i: (0, 0)

    def prompt_map(i):
        j = jnp.minimum(i, prompt_tiles - 1)
        return (j // tpb, 0, j % tpb)

    sample_map = lambda i: (0, jnp.maximum(i - prompt_tiles, 0))
    return pl.pallas_call(
        functools.partial(_proj_kernel, prompt_tiles=prompt_tiles),
        grid=(n // tm,),
        in_specs=[pl.BlockSpec((tm, d), row)] + [pl.BlockSpec(a.shape, fix) for a in (wq, wkt, wvt, wg, wft, bft)],
        out_specs=[pl.BlockSpec((tm, w), row), pl.BlockSpec((tm, wg.shape[1]), row),
                   pl.BlockSpec((None, w, tm), prompt_map), pl.BlockSpec((None, w, tm), prompt_map),
                   pl.BlockSpec((None, N_HEADS_B, tm), prompt_map),
                   pl.BlockSpec((w, tm), sample_map), pl.BlockSpec((w, tm), sample_map),
                   pl.BlockSpec((N_HEADS_B, tm), sample_map)],
        out_shape=[jax.ShapeDtypeStruct((n, w), BF16), jax.ShapeDtypeStruct((n, wg.shape[1]), F32),
                   jax.ShapeDtypeStruct((b, w, t), F32), jax.ShapeDtypeStruct((b, w, t), F32),
                   jax.ShapeDtypeStruct((b, N_HEADS_B, t), F32),
                   jax.ShapeDtypeStruct((w, n_s), F32), jax.ShapeDtypeStruct((w, n_s), F32),
                   jax.ShapeDtypeStruct((N_HEADS_B, n_s), F32)],
        compiler_params=_params("arbitrary"),
        name="project_in",
    )(x, wq, wkt, wvt, wg, wft, bft)


def _cumsum_kernel(x_ref, o_ref):
    o_ref[...] = _lane_cumsum(x_ref[...])


def _cumsum_rows(x):
    return pl.pallas_call(
        _cumsum_kernel,
        out_shape=jax.ShapeDtypeStruct(x.shape, F32),
        compiler_params=_params(),
        name="logf_cumsum",
    )(x)


def _pair_update(s0, s1, vt, lo, m0, l0, m1, l1, acc):
    m0n = jnp.maximum(m0, jnp.max(s0, axis=-1, keepdims=True))
    m1n = jnp.maximum(m1, jnp.max(s1, axis=-1, keepdims=True))
    a0 = jnp.exp(m0 - m0n)
    a1 = jnp.exp(m1 - m1n)
    p0 = jnp.exp(s0 - m0n)
    p1 = jnp.exp(s1 - m1n)
    l0 = a0 * l0 + jnp.sum(p0, axis=-1, keepdims=True)
    l1 = a1 * l1 + jnp.sum(p1, axis=-1, keepdims=True)
    pv = jnp.where(lo, _dot_nt(p0.astype(BF16), vt), _dot_nt(p1.astype(BF16), vt))
    acc = jnp.where(lo, a0, a1) * acc + pv
    return m0n, l0, m1n, l1, acc


def _fox_prompt_kernel(q_ref, kt_ref, vt_ref, c_ref, o_ref, kb_s, vb_s, *, tq):
    qi = pl.program_id(2)
    nkv = kb_s.shape[0]

    @pl.when(qi == 0)
    def _():
        for j in range(nkv):
            kb_s[j] = kt_ref[:, j * tq:(j + 1) * tq].astype(BF16)
            vb_s[j] = vt_ref[:, j * tq:(j + 1) * tq].astype(BF16)

    q0, q1, lo = _split_pair(q_ref[...])
    row = lax.broadcasted_iota(jnp.int32, (tq, tq), 0)
    col = lax.broadcasted_iota(jnp.int32, (tq, tq), 1)

    def logits(j):
        kt = kb_s[j]
        s0 = _dot(q0, kt) * SCALE - c_ref[0, pl.ds(j, 1), :]
        s1 = _dot(q1, kt) * SCALE - c_ref[1, pl.ds(j, 1), :]
        return s0, s1

    def past(j, carry):
        s0, s1 = logits(j)
        return _pair_update(s0, s1, vb_s[j], lo, *carry)

    col1 = lambda v: jnp.full((tq, 1), v, F32)
    init = (col1(MASKED), col1(0.0), col1(MASKED), col1(0.0), jnp.zeros((tq, PAIR), F32))
    carry = lax.fori_loop(0, qi, past, init)
    s0, s1 = logits(qi)
    causal = col <= row
    _, l0, _, l1, acc = _pair_update(jnp.where(causal, s0, MASKED), jnp.where(causal, s1, MASKED),
                                     vb_s[qi], lo, *carry)
    o_ref[...] = acc / jnp.where(lo, l0, l1)


def _fox_prompt(q, kt, vt, c4, b, t):
    tq = c4.shape[-1]
    nq = t // tq
    pairs = N_HEADS_B // 2
    first = N_HEADS_A // 2
    return pl.pallas_call(
        functools.partial(_fox_prompt_kernel, tq=tq),
        grid=(b, pairs, nq),
        in_specs=[pl.BlockSpec((tq, PAIR), lambda bi, p, qi: (bi * nq + qi, first + p)),
                  pl.BlockSpec((None, PAIR, t), lambda bi, p, qi: (bi, first + p, 0)),
                  pl.BlockSpec((None, PAIR, t), lambda bi, p, qi: (bi, first + p, 0)),
                  pl.BlockSpec((None, 2, nq, tq), lambda bi, p, qi: (bi, p, 0, 0))],
        out_specs=pl.BlockSpec((tq, PAIR), lambda bi, p, qi: (bi * nq + qi, p)),
        out_shape=jax.ShapeDtypeStruct((b * t, WIDTH_B), F32),
        scratch_shapes=[pltpu.VMEM((nq, PAIR, tq), BF16), pltpu.VMEM((nq, PAIR, tq), BF16)],
        compiler_params=_params("parallel", "parallel", "arbitrary"),
        name="fox_prompt",
    )(q, kt, vt, c4)


def _moba_prompt_kernel(slope_ref, q_ref, kt_ref, vt_ref, o_ref, kb_s, vb_s, km_s, m_s, l_s, acc_s, *, nb, n_sel):
    pair = pl.program_id(1)
    own = pl.program_id(2)
    blk = MOBA_BLOCK
    lane = lax.broadcasted_iota(jnp.int32, (1, LANES), 1)

    @pl.when(own == 0)
    def _():
        km = jnp.zeros((PAIR, LANES), F32)
        for n in range(nb):
            kn = kt_ref[:, n * blk:(n + 1) * blk]
            kb_s[n] = kn.astype(BF16)
            vb_s[n] = vt_ref[:, n * blk:(n + 1) * blk].astype(BF16)
            km = jnp.where(lane == n, jnp.sum(kn, axis=-1, keepdims=True) * (1.0 / blk), km)
        km_s[...] = km.T

    slope0 = slope_ref[2 * pair]
    slope1 = slope_ref[2 * pair + 1]
    q0, q1, lo = _split_pair(q_ref[...])
    km = km_s[:2 * SUBLANES, :].astype(BF16)
    sel0 = _rank_select_rows(_dot_nt(km, q0), own, nb - 1, n_sel)
    sel1 = _rank_select_rows(_dot_nt(km, q1), own, nb - 1, n_sel)
    row = lax.broadcasted_iota(jnp.int32, (blk, blk), 0)
    col = lax.broadcasted_iota(jnp.int32, (blk, blk), 1)
    pos = lax.broadcasted_iota(jnp.int32, (1, blk), 1).astype(F32)

    def logits(kt, first):
        p = pos + first
        return _dot(q0, kt) * SCALE + slope0 * p, _dot(q1, kt) * SCALE + slope1 * p

    s0, s1 = logits(kb_s[own], (own * blk).astype(F32))
    causal = col <= row
    col1 = lambda v: jnp.full((blk, 1), v, F32)
    m0, l0, m1, l1, acc = _pair_update(jnp.where(causal, s0, MASKED), jnp.where(causal, s1, MASKED), vb_s[own], lo,
                                       col1(MASKED), col1(0.0), col1(MASKED), col1(0.0), jnp.zeros((blk, PAIR), F32))
    m_s[0], l_s[0], m_s[1], l_s[1] = m0, l0, m1, l1
    acc_s[...] = acc

    for n in range(nb - 1):
        @pl.when(n < own)
        def _(n=n):
            s0, s1 = logits(kb_s[n], float(n * blk))
            s0 = jnp.where(sel0[:, n:n + 1] > 0.5, s0, MASKED)
            s1 = jnp.where(sel1[:, n:n + 1] > 0.5, s1, MASKED)
            m0, l0, m1, l1, acc = _pair_update(s0, s1, vb_s[n], lo, m_s[0], l_s[0], m_s[1], l_s[1], acc_s[...])
            m_s[0], l_s[0], m_s[1], l_s[1] = m0, l0, m1, l1
            acc_s[...] = acc

    o_ref[...] = acc_s[...] / jnp.where(lo, l_s[0], l_s[1])


def _moba_prompt(slopes, q, kt, vt, b, t):
    assert t % MOBA_BLOCK == 0
    nb = t // MOBA_BLOCK
    assert nb <= 2 * SUBLANES
    n_sel = min(MOBA_TOPK, nb - 1)
    blk = MOBA_BLOCK
    pairs = N_HEADS_A // 2
    return pl.pallas_call(
        functools.partial(_moba_prompt_kernel, nb=nb, n_sel=n_sel),
        grid=(b, pairs, nb),
        in_specs=[pl.BlockSpec(memory_space=pltpu.SMEM),
                  pl.BlockSpec((blk, PAIR), lambda bi, p, qi: (bi * nb + qi, p)),
                  pl.BlockSpec((None, PAIR, t), lambda bi, p, qi: (bi, p, 0)),
                  pl.BlockSpec((None, PAIR, t), lambda bi, p, qi: (bi, p, 0))],
        out_specs=pl.BlockSpec((blk, PAIR), lambda bi, p, qi: (bi * nb + qi, p)),
        out_shape=jax.ShapeDtypeStruct((b * t, WIDTH_A), F32),
        scratch_shapes=[pltpu.VMEM((nb, PAIR, blk), BF16), pltpu.VMEM((nb, PAIR, blk), BF16),
                        pltpu.VMEM((PAIR, LANES), F32),
                        pltpu.VMEM((2, blk, 1), F32), pltpu.VMEM((2, blk, 1), F32), pltpu.VMEM((blk, PAIR), F32)],
        compiler_params=_params("parallel", "parallel", "arbitrary"),
        name="moba_prompt",
    )(slopes, q, kt, vt)


def _sample_kernel(pt_ref, q_ref, knt_ref, vnt_ref, lfn_ref, slope_ref, *rest, n_blocks, bps, ppb, s_new, n_sel):
    del pt_ref
    npg = bps * ppb
    n_steps = n_blocks // bps
    kc, vc, lfc = rest[:npg], rest[npg:2 * npg], rest[2 * npg:3 * npg]
    o_ref, qbd_s, m_s, l_s, acc_s, km_s, tp_s = rest[3 * npg:]
    p = pl.program_id(1)
    rows = q_ref.shape[0]
    half = N_HEADS_A * rows
    page = LANES
    lane = lax.broadcasted_iota(jnp.int32, (1, LANES), 1)
    lo = lane < HEAD_DIM

    @pl.when(p == 0)
    def _():
        q = q_ref[...].astype(F32)
        col_head = jnp.right_shift(lax.broadcasted_iota(jnp.int32, (1, q.shape[1]), 1), HEAD_DIM.bit_length() - 1)
        qbd_s[...] = jnp.concatenate([jnp.where(col_head == h, q, 0.0) for h in range(N_HEADS)], axis=0).astype(BF16)
        m_s[...] = jnp.full(m_s.shape, MASKED, F32)
        l_s[...] = jnp.zeros(l_s.shape, F32)
        km_s[...] = jnp.zeros(km_s.shape, F32)
        tp_s[...] = jnp.zeros(tp_s.shape, F32)

    def head_rows(x):
        return jnp.concatenate([jnp.broadcast_to(x[h:h + 1], (rows, x.shape[1])) for h in range(x.shape[0])], axis=0)

    def pair_rows(wide):
        return jnp.concatenate([jnp.where(lo, wide[2 * rows * j:2 * rows * j + rows],
                                          wide[2 * rows * j + rows:2 * rows * (j + 1)])
                                for j in range(N_HEADS // 2)], axis=0)

    def block_partial(kts, vts, bias, mask, slot):
        qbd = qbd_s[...]
        s = jnp.concatenate([_dot(qbd, kt) for kt in kts], axis=1) * SCALE + bias
        if mask is not None:
            s = jnp.where(mask, s, MASKED)
        m = jnp.max(s, axis=-1, keepdims=True)
        e = jnp.exp(s - m)
        l = jnp.sum(e, axis=-1, keepdims=True)
        eb = e.astype(BF16)
        outs = []
        for j in range(N_HEADS // 2):
            r = None
            for a, vt in enumerate(vts):
                term = _dot_nt(eb[2 * rows * j:2 * rows * (j + 1), a * LANES:(a + 1) * LANES],
                               vt[PAIR * j:PAIR * (j + 1), :])
                r = term if r is None else r + term
            outs.append(jnp.where(lo, r[:rows], r[rows:]))
        acc_s[pl.ds(slot, 1)] = jnp.concatenate(outs, axis=0)[None]
        m_s[...] = jnp.where(lane == slot, m, m_s[...])
        l_s[...] = jnp.where(lane == slot, l, l_s[...])

    @pl.when(p < n_steps)
    def _():
        base = tp_s[...]
        ksums = None
        for bi in range(bps):
            slot = p * bps + bi
            cbs, kts, vts = [], [], []
            kacc = None
            for a in range(bi * ppb, (bi + 1) * ppb):
                pre = _lane_cumsum(lfc[a][...])
                cbs.append(-(base + pre))
                base = base + pre[:, page - 1:page]
                kf = kc[a][...]
                kts.append(kf.astype(BF16))
                vts.append(vc[a][...].astype(BF16))
                kacc = kf[:WIDTH_A] if kacc is None else kacc + kf[:WIDTH_A]
            ksum = jnp.sum(kacc, axis=-1, keepdims=True) * jnp.where(lane == slot, 1.0, 0.0)
            ksums = ksum if ksums is None else ksums + ksum
            pos = (slot * MOBA_BLOCK + lax.broadcasted_iota(jnp.int32, (1, ppb * page), 1)).astype(F32)
            bias = jnp.concatenate([slope_ref[:half, :1] * pos, head_rows(jnp.concatenate(cbs, axis=1))], axis=0)
            block_partial(kts, vts, bias, None, slot)
        tp_s[...] = base
        km_s[...] = km_s[...] + ksums

    @pl.when(p == n_steps)
    def _():
        cb = -(tp_s[...] + _lane_cumsum(lfn_ref[...]))
        pos = (n_blocks * MOBA_BLOCK + lane).astype(F32)
        bias = jnp.concatenate([slope_ref[:half, :1] * pos, head_rows(cb)], axis=0)
        nrow = N_HEADS * rows
        r_i = jnp.bitwise_and(lax.broadcasted_iota(jnp.int32, (nrow, LANES), 0), rows - 1)
        c_i = lax.broadcasted_iota(jnp.int32, (nrow, LANES), 1)
        block_partial([knt_ref[...]], [vnt_ref[...]], bias, (c_i <= r_i) & (c_i < s_new), n_blocks)

        kmean = (km_s[...] * (1.0 / MOBA_BLOCK)).astype(BF16)
        gate = _dot(qbd_s[:half, :WIDTH_A], kmean)
        sel = jnp.where(_rank_select(gate, n_blocks, n_blocks, n_sel) | (lane == n_blocks), 1.0, 0.0)
        fox = jnp.broadcast_to(jnp.where(lane <= n_blocks, 1.0, 0.0), (nrow - half, LANES))
        m_eff = jnp.where(jnp.concatenate([sel, fox], axis=0) > 0.5, m_s[...], MASKED)
        w = jnp.exp(m_eff - jnp.max(m_eff, axis=-1, keepdims=True))
        l = jnp.sum(w * l_s[...], axis=-1, keepdims=True)
        acc = jnp.zeros(acc_s.shape[1:], F32)
        for slot in range(n_blocks + 1):
            acc = acc + pair_rows(jnp.broadcast_to(w[:, slot:slot + 1], (nrow, LANES))) * acc_s[slot]
        out = acc / pair_rows(jnp.broadcast_to(l, (nrow, LANES)))
        o_ref[...] = jnp.concatenate([out[rows * j:rows * (j + 1)] for j in range(N_HEADS // 2)], axis=1)


def _sample_attention(layer, page_table, s_new, q, knt, vnt, lfn, slope_rows, cache_kt, cache_vt, cache_lft):
    bs, rows, d = q.shape
    n_pages = page_table.shape[1]
    page = cache_kt.shape[-1]
    assert page == LANES and MOBA_BLOCK % page == 0 and rows == SUBLANES
    ppb = MOBA_BLOCK // page
    assert n_pages % ppb == 0, "past length must be a multiple of the MoBA block"
    n_blocks = n_pages // ppb
    n_sel = min(MOBA_TOPK, n_blocks)
    assert 0 < n_blocks < LANES
    bps = 2 if n_blocks % 2 == 0 else 1
    n_steps = n_blocks // bps
    npg = bps * ppb

    def cache_map(a):
        return lambda b, p, pt: (layer, pt[b * n_pages + jnp.minimum(p, n_steps - 1) * npg + a], 0, 0)

    seq = lambda b, p, pt: (b, 0, 0)
    nrow = N_HEADS * rows
    grid_spec = pltpu.PrefetchScalarGridSpec(
        num_scalar_prefetch=1,
        grid=(bs, n_steps + 1),
        in_specs=[pl.BlockSpec((None, rows, d), seq),
                  pl.BlockSpec((None, d, LANES), seq),
                  pl.BlockSpec((None, d, LANES), seq),
                  pl.BlockSpec((None, N_HEADS_B, LANES), seq),
                  pl.BlockSpec((nrow, LANES), lambda b, p, pt: (0, 0))]
                 + [pl.BlockSpec((None, None, d, page), cache_map(a)) for a in range(npg)]
                 + [pl.BlockSpec((None, None, d, page), cache_map(a)) for a in range(npg)]
                 + [pl.BlockSpec((None, None, N_HEADS_B, page), cache_map(a)) for a in range(npg)],
        out_specs=pl.BlockSpec((None, rows, d), seq),
        scratch_shapes=[pltpu.VMEM((nrow, d), BF16),
                        pltpu.VMEM((nrow, LANES), F32), pltpu.VMEM((nrow, LANES), F32),
                        pltpu.VMEM((n_blocks + 1, nrow // 2, LANES), F32),
                        pltpu.VMEM((WIDTH_A, LANES), F32), pltpu.VMEM((N_HEADS_B, LANES), F32)],
    )
    return pl.pallas_call(
        functools.partial(_sample_kernel, n_blocks=n_blocks, bps=bps, ppb=ppb, s_new=s_new, n_sel=n_sel),
        grid_spec=grid_spec,
        out_shape=jax.ShapeDtypeStruct((bs, rows, d), F32),
        compiler_params=_params("parallel", "arbitrary"),
        name="sample_attention",
    )(page_table.reshape(-1), q, knt, vnt, lfn, slope_rows,
      *([cache_kt] * npg), *([cache_vt] * npg), *([cache_lft] * npg))


def _merge_kernel(oa_ref, ob_ref, g_ref, x_ref, wpa_ref, wpb_ref, wo_ref, lg_ref, lb_ref, o_ref, *, alpha):
    d = x_ref.shape[1]
    ya = _dot(oa_ref[...].astype(BF16), wpa_ref[...])
    yb = _dot(ob_ref[...].astype(BF16), wpb_ref[...])
    mix = _sigmoid(g_ref[:, :d]) * ya + _sigmoid(g_ref[:, d:]) * yb
    y = _dot(mix.astype(BF16), wo_ref[...])
    o_ref[...] = _layer_norm(alpha * x_ref[...] + y, lg_ref[...], lb_ref[...])


def _merge_norm(oa, ob, g, x, wpa, wpb, wo, lg, lb, alpha):
    n, d = x.shape
    tm = _pick_tile(n, 256)
    row = lambda i: (i, 0)
    fix = lambda i: (0, 0)
    return pl.pallas_call(
        functools.partial(_merge_kernel, alpha=alpha),
        grid=(n // tm,),
        in_specs=[pl.BlockSpec((tm, oa.shape[1]), row), pl.BlockSpec((tm, ob.shape[1]), row),
                  pl.BlockSpec((tm, g.shape[1]), row), pl.BlockSpec((tm, d), row)]
                 + [pl.BlockSpec(w.shape, fix) for w in (wpa, wpb, wo, lg, lb)],
        out_specs=pl.BlockSpec((tm, d), row),
        out_shape=jax.ShapeDtypeStruct((n, d), F32),
        compiler_params=_params("parallel"),
        name="merge_norm",
    )(oa, ob, g, x, wpa, wpb, wo, lg, lb)


def _swiglu_kernel(te_ref, nv_ref, x_ref, wg_ref, wu_ref, wd_ref, lg_ref, lb_ref, o_ref, xb_s, acc_s,
                   *, nj, fuse_norm, alpha):
    del te_ref
    i = pl.program_id(0)
    j = pl.program_id(1)
    live = i < nv_ref[0]

    @pl.when(live)
    def _():
        @pl.when(j == 0)
        def _():
            xb_s[...] = x_ref[...].astype(BF16)

        xb = xb_s[...]
        hg = _dot(xb, wg_ref[...])
        hu = _dot(xb, wu_ref[...])
        part = _dot((hg * _sigmoid(hg) * hu).astype(BF16), wd_ref[...])

        @pl.when(j == 0)
        def _():
            acc_s[...] = part

        @pl.when(j > 0)
        def _():
            acc_s[...] = acc_s[...] + part

        @pl.when(j == nj - 1)
        def _():
            if fuse_norm:
                o_ref[...] = _layer_norm(alpha * x_ref[...] + acc_s[...], lg_ref[...], lb_ref[...])
            else:
                o_ref[...] = acc_s[...]

    @pl.when(jnp.logical_not(live) & (j == nj - 1))
    def _():
        o_ref[...] = jnp.zeros(o_ref.shape, F32)


def _swiglu(tile_expert, n_live, x, wg, wu, wd, lg, lb, *, tm, tf, fuse_norm, alpha):
    n, d = x.shape
    f = wg.shape[2]
    assert n % tm == 0 and f % tf == 0
    nj = f // tf
    grid_spec = pltpu.PrefetchScalarGridSpec(
        num_scalar_prefetch=2,
        grid=(n // tm, nj),
        in_specs=[pl.BlockSpec((tm, d), lambda i, j, te, nv: (i, 0)),
                  pl.BlockSpec((None, d, tf), lambda i, j, te, nv: (te[i], 0, j)),
                  pl.BlockSpec((None, d, tf), lambda i, j, te, nv: (te[i], 0, j)),
                  pl.BlockSpec((None, tf, d), lambda i, j, te, nv: (te[i], j, 0)),
                  pl.BlockSpec((1, d), lambda i, j, te, nv: (0, 0)),
                  pl.BlockSpec((1, d), lambda i, j, te, nv: (0, 0))],
        out_specs=pl.BlockSpec((tm, d), lambda i, j, te, nv: (i, 0)),
        scratch_shapes=[pltpu.VMEM((tm, d), BF16), pltpu.VMEM((tm, d), F32)],
    )
    return pl.pallas_call(
        functools.partial(_swiglu_kernel, nj=nj, fuse_norm=fuse_norm, alpha=alpha),
        grid_spec=grid_spec,
        out_shape=jax.ShapeDtypeStruct((n, d), F32),
        compiler_params=_params("parallel", "arbitrary"),
        name="swiglu_norm" if fuse_norm else "swiglu_grouped",
    )(tile_expert, n_live, x, wg, wu, wd, lg, lb)


def _hidden_tile(f, cap):
    best = None
    for tf in range(LANES, min(f, cap) + 1, LANES):
        if f % tf == 0:
            best = tf
    assert best is not None, f
    return best


def _router_kernel(x_ref, wr_ref, br_ref, o_ref, cnt_ref, carry_s, *, n_experts):
    i = pl.program_id(0)
    tm = x_ref.shape[0]

    @pl.when(i == 0)
    def _():
        carry_s[...] = jnp.zeros(carry_s.shape, F32)

    lane = lax.broadcasted_iota(jnp.int32, (tm, LANES), 1)
    logits = _dot(x_ref[...].astype(BF16), wr_ref[...]) + br_ref[...]
    logits = jnp.where(lane < n_experts, logits, MASKED)
    v1 = jnp.max(logits, axis=-1, keepdims=True)
    i1 = jnp.min(jnp.where(logits == v1, lane, LANES), axis=-1, keepdims=True)
    rest = jnp.where(lane == i1, MASKED, logits)
    v2 = jnp.max(rest, axis=-1, keepdims=True)
    i2 = jnp.min(jnp.where(rest == v2, lane, LANES), axis=-1, keepdims=True)
    e = jnp.exp(v2 - v1)
    g1 = 1.0 / (1.0 + e)
    g2 = e / (1.0 + e)
    oh1 = jnp.where(lane == i1, 1.0, 0.0)
    oh2 = jnp.where(lane == i2, 1.0, 0.0)
    both = oh1 + oh2
    r_i = lax.broadcasted_iota(jnp.int32, (tm, tm), 0)
    c_i = lax.broadcasted_iota(jnp.int32, (tm, tm), 1)
    earlier = jnp.where(c_i < r_i, 1.0, 0.0).astype(BF16)
    before = carry_s[...] + _dot(earlier, both.astype(BF16))
    r1 = jnp.sum(oh1 * before, axis=-1, keepdims=True)
    r2 = jnp.sum(oh2 * before, axis=-1, keepdims=True)
    carry_s[...] = carry_s[...] + jnp.sum(both, axis=0, keepdims=True)
    out = jnp.where(lane == 0, i1.astype(F32), 0.0)
    out = jnp.where(lane == 1, i2.astype(F32), out)
    out = jnp.where(lane == 2, g1, out)
    out = jnp.where(lane == 3, g2, out)
    out = jnp.where(lane == 4, r1, out)
    out = jnp.where(lane == 5, r2, out)
    o_ref[...] = out
    cnt_ref[...] = carry_s[...]


def _route(x, wr, br, n_experts):
    n, d = x.shape
    tm = _pick_tile(n, 512)
    return pl.pallas_call(
        functools.partial(_router_kernel, n_experts=n_experts),
        grid=(n // tm,),
        in_specs=[pl.BlockSpec((tm, d), lambda i: (i, 0)),
                  pl.BlockSpec(wr.shape, lambda i: (0, 0)),
                  pl.BlockSpec(br.shape, lambda i: (0, 0))],
        out_specs=[pl.BlockSpec((tm, LANES), lambda i: (i, 0)), pl.BlockSpec((1, LANES), lambda i: (0, 0))],
        out_shape=[jax.ShapeDtypeStruct((n, LANES), F32), jax.ShapeDtypeStruct((1, LANES), F32)],
        scratch_shapes=[pltpu.VMEM((1, LANES), F32)],
        compiler_params=_params("arbitrary"),
        name="moe_route",
    )(x, wr, br)


def _row_copy(src, dst, sem):
    return pltpu.make_async_copy(src, dst, sem)


def _dispatch_kernel(dest_ref, x_ref, init_ref, o_ref, sem):
    del init_ref
    ts = x_ref.shape[0]

    def issue(r, c):
        for k in range(TOP_K):
            _row_copy(x_ref.at[pl.ds(r, 1)], o_ref.at[pl.ds(dest_ref[TOP_K * r + k], 1)], sem).start()
        return c

    lax.fori_loop(0, ts, issue, 0, unroll=SUBLANES)

    def drain(r, c):
        for k in range(TOP_K):
            _row_copy(x_ref.at[pl.ds(0, 1)], o_ref.at[pl.ds(0, 1)], sem).wait()
        return c

    lax.fori_loop(0, ts, drain, 0)


def _dispatch(dest, x, n_slots):
    n, d = x.shape
    ts = _pick_tile(n, 256)
    init = jnp.zeros((n_slots, d), F32)
    return pl.pallas_call(
        _dispatch_kernel,
        grid=(n // ts,),
        in_specs=[pl.BlockSpec((TOP_K * ts,), lambda i: (i,), memory_space=pltpu.SMEM),
                  pl.BlockSpec((ts, d), lambda i: (i, 0)),
                  pl.BlockSpec(memory_space=pl.ANY)],
        out_specs=pl.BlockSpec(memory_space=pl.ANY),
        out_shape=jax.ShapeDtypeStruct((n_slots, d), F32),
        scratch_shapes=[pltpu.SemaphoreType.DMA(())],
        input_output_aliases={2: 0},
        compiler_params=_params("arbitrary"),
        name="moe_dispatch",
    )(dest, x, init)


def _combine_kernel(dest_ref, r_ref, x_ref, y_ref, lg_ref, lb_ref, o_ref, buf, sem, *, alpha):
    ts = x_ref.shape[0]

    def issue(r, c):
        for k in range(TOP_K):
            _row_copy(y_ref.at[pl.ds(dest_ref[TOP_K * r + k], 1)], buf.at[k, pl.ds(r, 1)], sem).start()
        return c

    lax.fori_loop(0, ts, issue, 0, unroll=SUBLANES)

    def drain(r, c):
        for k in range(TOP_K):
            _row_copy(y_ref.at[pl.ds(0, 1)], buf.at[k, pl.ds(0, 1)], sem).wait()
        return c

    lax.fori_loop(0, ts, drain, 0)
    y = r_ref[:, 2:3] * buf[0] + r_ref[:, 3:4] * buf[1]
    o_ref[...] = _layer_norm(alpha * x_ref[...] + y, lg_ref[...], lb_ref[...])


def _combine_norm(dest, routing, x, y_sorted, lg, lb, alpha):
    n, d = x.shape
    ts = _pick_tile(n, 256)
    return pl.pallas_call(
        functools.partial(_combine_kernel, alpha=alpha),
        grid=(n // ts,),
        in_specs=[pl.BlockSpec((TOP_K * ts,), lambda i: (i,), memory_space=pltpu.SMEM),
                  pl.BlockSpec((ts, LANES), lambda i: (i, 0)),
                  pl.BlockSpec((ts, d), lambda i: (i, 0)),
                  pl.BlockSpec(memory_space=pl.ANY),
                  pl.BlockSpec((1, d), lambda i: (0, 0)),
                  pl.BlockSpec((1, d), lambda i: (0, 0))],
        out_specs=pl.BlockSpec((ts, d), lambda i: (i, 0)),
        out_shape=jax.ShapeDtypeStruct((n, d), F32),
        scratch_shapes=[pltpu.VMEM((TOP_K, ts, d), F32), pltpu.SemaphoreType.DMA(())],
        compiler_params=_params("arbitrary"),
        name="moe_combine_norm",
    )(dest, routing, x, y_sorted, lg, lb)


def _moe_norm(x, wr, br, wg, wu, wd, first_expert, n_experts, lg, lb, alpha):
    n, d = x.shape
    tm = _pick_tile(n, 512)
    routing, counts = _route(x, wr, br, n_experts)
    counts = counts[0, :n_experts].astype(jnp.int32)
    tiles = (counts + tm - 1) // tm
    tile_end = jnp.cumsum(tiles)
    group_start = (tile_end - tiles) * tm
    expert = routing[:, :TOP_K].astype(jnp.int32)
    rank = routing[:, 4:4 + TOP_K].astype(jnp.int32)
    dest = (group_start[expert] + rank).reshape(-1)
    n_tiles = (TOP_K * n) // tm + n_experts
    ended = (tile_end[None, :] <= jnp.arange(n_tiles)[:, None]).astype(jnp.int32)
    tile_expert = (first_expert + jnp.minimum(jnp.sum(ended, axis=1), n_experts - 1)).astype(jnp.int32)
    n_live = tile_end[-1:].astype(jnp.int32)
    x_sorted = _dispatch(dest, x, n_tiles * tm)
    y_sorted = _swiglu(tile_expert, n_live, x_sorted, wg, wu, wd, lg, lb,
                       tm=tm, tf=_hidden_tile(wg.shape[2], 1024), fuse_norm=False, alpha=alpha)
    return _combine_norm(dest, routing, x, y_sorted, lg, lb, alpha)


def _dense_norm(x, wg, wu, wd, index, lg, lb, alpha):
    n, d = x.shape
    tm = _pick_tile(n, 512)
    tile_expert = jnp.full((n // tm,), index, jnp.int32)
    n_live = jnp.full((1,), n // tm, jnp.int32)
    return _swiglu(tile_expert, n_live, x, wg, wu, wd, lg, lb,
                   tm=tm, tf=_hidden_tile(wg.shape[2], 1536), fuse_norm=True, alpha=alpha)


def kernel(x_prompt, x_sample, cache_k, cache_v, cache_logf, page_table, w_in, b_f, w_pa, w_pb, w_o,
           ln1_g, ln1_b, ln2_g, ln2_b, w_gate_d, w_up_d, w_down_d, w_router, b_router,
           w_gate_e, w_up_e, w_down_e):
    depth, d_model, _ = w_in.shape
    b, t, _ = x_prompt.shape
    bs, s_new, _ = x_sample.shape
    n_p = b * t
    n_s = bs * s_new
    alpha = (2.0 * depth) ** 0.25
    n_pool, page = cache_k.shape[1], cache_k.shape[2]
    rows = SUBLANES
    width = WIDTH_A + WIDTH_B
    assert s_new <= rows and d_model == width
    tq = MOBA_BLOCK
    tm = 256 if n_s % 256 == 0 else LANES

    o_qa, o_ka, o_va = 0, WIDTH_A, 2 * WIDTH_A
    o_qb = 3 * WIDTH_A
    o_kb, o_vb = o_qb + WIDTH_B, o_qb + 2 * WIDTH_B
    o_f = o_qb + 3 * WIDTH_B
    o_ga = o_f + N_HEADS_B
    o_gb = o_ga + d_model

    def cols(w, *spans):
        return jnp.concatenate([w[:, lo:lo + n] for lo, n in spans], axis=1).astype(BF16)

    slopes = 2.0 ** (-8.0 * jnp.arange(1, N_HEADS_A + 1, dtype=F32) / N_HEADS_A)
    slope_rows = jnp.broadcast_to(
        jnp.repeat(jnp.concatenate([slopes, jnp.zeros((N_HEADS_B,), F32)]), rows)[:, None], (N_HEADS * rows, LANES))
    cache_kt = jnp.transpose(cache_k, (0, 1, 3, 4, 2)).reshape(depth, n_pool, width, page)
    cache_vt = jnp.transpose(cache_v, (0, 1, 3, 4, 2)).reshape(depth, n_pool, width, page)
    cache_lft = jnp.swapaxes(cache_logf, 2, 3)
    wgd, wud, wdd = w_gate_d.astype(BF16), w_up_d.astype(BF16), w_down_d.astype(BF16)
    n_experts = w_router.shape[2]
    flat = lambda w: w.astype(BF16).reshape((-1,) + w.shape[2:])
    wge, wue, wde = flat(w_gate_e), flat(w_up_e), flat(w_down_e)

    x = jnp.concatenate([x_prompt.reshape(n_p, d_model), x_sample.reshape(n_s, d_model)], axis=0)
    ktp, vtp, ftp, kts, vts, fts = [], [], [], [], [], []
    for l in range(depth):
        wl = w_in[l]
        wq = cols(wl, (o_qa, WIDTH_A), (o_qb, WIDTH_B))
        wkt = cols(wl, (o_ka, WIDTH_A), (o_kb, WIDTH_B)).T
        wvt = cols(wl, (o_va, WIDTH_A), (o_vb, WIDTH_B)).T
        wg = cols(wl, (o_ga, d_model), (o_gb, d_model))
        wft = jnp.pad(wl[:, o_f:o_ga].T, ((0, 2 * SUBLANES - N_HEADS_B), (0, 0))).astype(BF16)
        q, g, kt_p, vt_p, ft_p, kt_s, vt_s, ft_s = _project_in(
            x, wq, wkt, wvt, wg, wft, b_f[l].reshape(N_HEADS_B, 1), b, t, n_s, tm)
        for acc, a in zip((ktp, vtp, ftp, kts, vts, fts), (kt_p, vt_p, ft_p, kt_s, vt_s, ft_s)):
            acc.append(a)

        c = _cumsum_rows(ft_p.reshape(b * N_HEADS_B, t)).reshape(b, N_HEADS_B, t // tq, tq)
        oa = _moba_prompt(slopes, q, kt_p, vt_p, b, t)
        ob = _fox_prompt(q, kt_p, vt_p, c, b, t)

        def new_tokens(a, dtype):
            a = a.reshape(a.shape[0], bs, s_new).transpose(1, 0, 2).astype(dtype)
            return jnp.pad(a, ((0, 0), (0, 0), (0, LANES - s_new)))

        q_s = jnp.pad(q[n_p:].reshape(bs, s_new, width), ((0, 0), (0, rows - s_new), (0, 0)))
        o_s = _sample_attention(l, page_table, s_new, q_s, new_tokens(kt_s, BF16), new_tokens(vt_s, BF16),
                                new_tokens(ft_s, F32), slope_rows, cache_kt, cache_vt, cache_lft)
        o_s = o_s[:, :s_new].reshape(n_s, width)
        oa = jnp.concatenate([oa, o_s[:, :WIDTH_A]], axis=0)
        ob = jnp.concatenate([ob, o_s[:, WIDTH_A:]], axis=0)

        x = _merge_norm(oa, ob, g, x, w_pa[l].astype(BF16), w_pb[l].astype(BF16), w_o[l].astype(BF16),
                        ln1_g[l].reshape(1, -1), ln1_b[l].reshape(1, -1), alpha)

        i = l // 2
        lg, lb = ln2_g[l].reshape(1, -1), ln2_b[l].reshape(1, -1)
        if l % 2 == 0:
            x = _dense_norm(x, wgd, wud, wdd, i, lg, lb, alpha)
        else:
            wr = jnp.pad(w_router[i], ((0, 0), (0, LANES - n_experts))).astype(BF16)
            br = jnp.pad(b_router[i], (0, LANES - n_experts)).reshape(1, LANES)
            x = _moe_norm(x, wr, br, wge, wue, wde, i * n_experts, n_experts, lg, lb, alpha)

    def prompt_heads(parts, heads):
        a = jnp.stack(parts).reshape(depth, b, heads, -1, t)
        return jnp.transpose(a, (0, 1, 4, 2, 3))

    def sample_heads(parts, heads):
        a = jnp.stack(parts).reshape(depth, heads, -1, bs, s_new)
        return jnp.transpose(a, (0, 3, 4, 1, 2))

    return (x[:n_p].reshape(b, t, d_model), x[n_p:].reshape(bs, s_new, d_model),
            prompt_heads(ktp, N_HEADS), prompt_heads(vtp, N_HEADS),
            prompt_heads(ftp, N_HEADS_B).reshape(depth, b, t, N_HEADS_B),
            sample_heads(kts, N_HEADS), sample_heads(vts, N_HEADS),
            sample_heads(fts, N_HEADS_B).reshape(depth, bs, s_new, N_HEADS_B))
```

```python
import functools

import jax
import jax.numpy as jnp
from jax import lax
from jax.experimental import pallas as pl
from jax.experimental.pallas import tpu as pltpu

F32 = jnp.float32
BF16 = jnp.bfloat16

HEAD_DIM = 64
N_HEADS_A = 8
N_HEADS_B = 8
N_HEADS = N_HEADS_A + N_HEADS_B
WIDTH_A = N_HEADS_A * HEAD_DIM
WIDTH_B = N_HEADS_B * HEAD_DIM
MOBA_BLOCK = 256
MOBA_TOPK = 3
TOP_K = 2
LN_EPS = 1e-5
SCALE = HEAD_DIM ** -0.5
assert SCALE == 0.125
MASKED = -1e30

LANES = 128
SUBLANES = 8
PAIR = 2 * HEAD_DIM
assert PAIR == LANES
VMEM_LIMIT_BYTES = 56 * 1024 * 1024


def _params(*sem):
    return pltpu.CompilerParams(dimension_semantics=sem, vmem_limit_bytes=VMEM_LIMIT_BYTES)


def _pick_tile(n, cap):
    t = cap
    while t > SUBLANES and n % t:
        t //= 2
    assert n % t == 0, (n, cap)
    return t


def _dot(a, b):
    return jnp.dot(a, b, preferred_element_type=F32)


def _dot_nt(a, b):
    return lax.dot_general(a, b, (((1,), (1,)), ((), ())), preferred_element_type=F32)


def _sigmoid(x):
    return 1.0 / (1.0 + jnp.exp(-x))


def _layer_norm(y, g, b):
    mu = jnp.mean(y, axis=-1, keepdims=True)
    d = y - mu
    var = jnp.mean(d * d, axis=-1, keepdims=True)
    return d * lax.rsqrt(var + LN_EPS) * g + b


def _lane_cumsum(x):
    n = x.shape[-1]
    lane = lax.broadcasted_iota(jnp.int32, x.shape, x.ndim - 1)
    sh = 1
    while sh < n:
        x = x + jnp.where(lane >= sh, pltpu.roll(x, sh, x.ndim - 1), 0.0)
        sh *= 2
    return x


def _rank_select(gate, n_valid, n_cand, n_sel):
    lane = lax.broadcasted_iota(jnp.int32, gate.shape, 1)
    gate = jnp.where(lane < n_valid, gate, MASKED)
    beaten = jnp.zeros(gate.shape, F32)
    for m in range(n_cand):
        gm = gate[:, m:m + 1]
        beaten = beaten + jnp.where((gm > gate) | ((gm == gate) & (m < lane)), 1.0, 0.0)
    return (beaten < n_sel) & (lane < n_valid)


def _rank_select_rows(gate, n_valid, n_cand, n_sel):
    row = lax.broadcasted_iota(jnp.int32, gate.shape, 0)
    gate = jnp.where(row < n_valid, gate, MASKED)
    beaten = jnp.zeros(gate.shape, F32)
    for m in range(n_cand):
        gm = gate[m:m + 1, :]
        beaten = beaten + jnp.where((gm > gate) | ((gm == gate) & (m < row)), 1.0, 0.0)
    sel = jnp.where((beaten < n_sel) & (row < n_valid), 1.0, 0.0)
    pad = jnp.zeros((LANES - gate.shape[0], gate.shape[1]), F32)
    return jnp.concatenate([sel, pad], axis=0).T


def _split_pair(q):
    lo = lax.broadcasted_iota(jnp.int32, (1, PAIR), 1) < HEAD_DIM
    zero = jnp.zeros_like(q)
    return jnp.where(lo, q, zero), jnp.where(lo, zero, q), lo


def _proj_kernel(x_ref, wq_ref, wkt_ref, wvt_ref, wg_ref, wft_ref, bft_ref,
                 q_ref, g_ref, ktp_ref, vtp_ref, ftp_ref, kts_ref, vts_ref, fts_ref, *, prompt_tiles):
    i = pl.program_id(0)
    x = x_ref[...].astype(BF16)
    q_ref[...] = _dot(x, wq_ref[...]).astype(BF16)
    g_ref[...] = _dot(x, wg_ref[...])
    kt = _dot_nt(wkt_ref[...], x)
    vt = _dot_nt(wvt_ref[...], x)
    z = _dot_nt(wft_ref[...], x)[:N_HEADS_B] + bft_ref[...]
    ft = jnp.minimum(z, 0.0) - jnp.log(1.0 + jnp.exp(-jnp.abs(z)))

    @pl.when(i < prompt_tiles)
    def _():
        ktp_ref[...] = kt
        vtp_ref[...] = vt
        ftp_ref[...] = ft

    @pl.when(i >= prompt_tiles)
    def _():
        kts_ref[...] = kt
        vts_ref[...] = vt
        fts_ref[...] = ft


def _project_in(x, wq, wkt, wvt, wg, wft, bft, b, t, n_s, tm):
    n, d = x.shape
    w = wkt.shape[0]
    assert t % tm == 0 and n_s % tm == 0
    tpb = t // tm
    prompt_tiles = b * tpb
    row = lambda i: (i, 0)
    fix = lambda i: (0, 0)

    def prompt_map(i):
        j = jnp.minimum(i, prompt_tiles - 1)
        return (j // tpb, 0, j % tpb)

    sample_map = lambda i: (0, jnp.maximum(i - prompt_tiles, 0))
    return pl.pallas_call(
        functools.partial(_proj_kernel, prompt_tiles=prompt_tiles),
        grid=(n // tm,),
        in_specs=[pl.BlockSpec((tm, d), row)] + [pl.BlockSpec(a.shape, fix) for a in (wq, wkt, wvt, wg, wft, bft)],
        out_specs=[pl.BlockSpec((tm, w), row), pl.BlockSpec((tm, wg.shape[1]), row),
                   pl.BlockSpec((None, w, tm), prompt_map), pl.BlockSpec((None, w, tm), prompt_map),
                   pl.BlockSpec((None, N_HEADS_B, tm), prompt_map),
                   pl.BlockSpec((w, tm), sample_map), pl.BlockSpec((w, tm), sample_map),
                   pl.BlockSpec((N_HEADS_B, tm), sample_map)],
        out_shape=[jax.ShapeDtypeStruct((n, w), BF16), jax.ShapeDtypeStruct((n, wg.shape[1]), F32),
                   jax.ShapeDtypeStruct((b, w, t), F32), jax.ShapeDtypeStruct((b, w, t), F32),
                   jax.ShapeDtypeStruct((b, N_HEADS_B, t), F32),
                   jax.ShapeDtypeStruct((w, n_s), F32), jax.ShapeDtypeStruct((w, n_s), F32),
                   jax.ShapeDtypeStruct((N_HEADS_B, n_s), F32)],
        compiler_params=_params("arbitrary"),
        name="project_in",
    )(x, wq, wkt, wvt, wg, wft, bft)


def _cumsum_kernel(x_ref, o_ref):
    o_ref[...] = _lane_cumsum(x_ref[...])


def _cumsum_rows(x):
    return pl.pallas_call(
        _cumsum_kernel,
        out_shape=jax.ShapeDtypeStruct(x.shape, F32),
        compiler_params=_params(),
        name="logf_cumsum",
    )(x)


def _pair_update(s0, s1, vt, lo, m0, l0, m1, l1, acc):
    m0n = jnp.maximum(m0, jnp.max(s0, axis=-1, keepdims=True))
    m1n = jnp.maximum(m1, jnp.max(s1, axis=-1, keepdims=True))
    a0 = jnp.exp(m0 - m0n)
    a1 = jnp.exp(m1 - m1n)
    p0 = jnp.exp(s0 - m0n)
    p1 = jnp.exp(s1 - m1n)
    l0 = a0 * l0 + jnp.sum(p0, axis=-1, keepdims=True)
    l1 = a1 * l1 + jnp.sum(p1, axis=-1, keepdims=True)
    pv = jnp.where(lo, _dot_nt(p0.astype(BF16), vt), _dot_nt(p1.astype(BF16), vt))
    acc = jnp.where(lo, a0, a1) * acc + pv
    return m0n, l0, m1n, l1, acc


def _fox_prompt_kernel(q_ref, kt_ref, vt_ref, c_ref, o_ref, kb_s, vb_s, *, tq):
    qi = pl.program_id(2)
    nkv = kb_s.shape[0]

    @pl.when(qi == 0)
    def _():
        for j in range(nkv):
            kb_s[j] = kt_ref[:, j * tq:(j + 1) * tq].astype(BF16)
            vb_s[j] = vt_ref[:, j * tq:(j + 1) * tq].astype(BF16)

    q0, q1, lo = _split_pair(q_ref[...] * SCALE)
    row = lax.broadcasted_iota(jnp.int32, (tq, tq), 0)
    col = lax.broadcasted_iota(jnp.int32, (tq, tq), 1)

    def logits(j):
        kt = kb_s[j]
        s0 = _dot(q0, kt) - c_ref[0, pl.ds(j, 1), :]
        s1 = _dot(q1, kt) - c_ref[1, pl.ds(j, 1), :]
        return s0, s1

    def past(j, carry):
        s0, s1 = logits(j)
        return _pair_update(s0, s1, vb_s[j], lo, *carry)

    col1 = lambda v: jnp.full((tq, 1), v, F32)
    init = (col1(MASKED), col1(0.0), col1(MASKED), col1(0.0), jnp.zeros((tq, PAIR), F32))
    carry = lax.fori_loop(0, qi, past, init)
    s0, s1 = logits(qi)
    causal = col <= row
    _, l0, _, l1, acc = _pair_update(jnp.where(causal, s0, MASKED), jnp.where(causal, s1, MASKED),
                                     vb_s[qi], lo, *carry)
    o_ref[...] = acc / jnp.where(lo, l0, l1)


def _fox_prompt(q, kt, vt, c4, b, t):
    tq = c4.shape[-1]
    nq = t // tq
    pairs = N_HEADS_B // 2
    first = N_HEADS_A // 2
    return pl.pallas_call(
        functools.partial(_fox_prompt_kernel, tq=tq),
        grid=(b, pairs, nq),
        in_specs=[pl.BlockSpec((tq, PAIR), lambda bi, p, qi: (bi * nq + qi, first + p)),
                  pl.BlockSpec((None, PAIR, t), lambda bi, p, qi: (bi, first + p, 0)),
                  pl.BlockSpec((None, PAIR, t), lambda bi, p, qi: (bi, first + p, 0)),
                  pl.BlockSpec((None, 2, nq, tq), lambda bi, p, qi: (bi, p, 0, 0))],
        out_specs=pl.BlockSpec((tq, PAIR), lambda bi, p, qi: (bi * nq + qi, p)),
        out_shape=jax.ShapeDtypeStruct((b * t, WIDTH_B), F32),
        scratch_shapes=[pltpu.VMEM((nq, PAIR, tq), BF16), pltpu.VMEM((nq, PAIR, tq), BF16)],
        compiler_params=_params("parallel", "parallel", "arbitrary"),
        name="fox_prompt",
    )(q, kt, vt, c4)


def _moba_prompt_kernel(slope_ref, q_ref, kt_ref, vt_ref, o_ref, kb_s, vb_s, km_s, m_s, l_s, acc_s, *, nb, n_sel):
    pair = pl.program_id(1)
    own = pl.program_id(2)
    blk = MOBA_BLOCK
    lane = lax.broadcasted_iota(jnp.int32, (1, LANES), 1)

    @pl.when(own == 0)
    def _():
        km = jnp.zeros((PAIR, LANES), F32)
        for n in range(nb):
            kn = kt_ref[:, n * blk:(n + 1) * blk]
            kb_s[n] = kn.astype(BF16)
            vb_s[n] = vt_ref[:, n * blk:(n + 1) * blk].astype(BF16)
            km = jnp.where(lane == n, jnp.sum(kn, axis=-1, keepdims=True) * (1.0 / blk), km)
        km_s[...] = km.T

    slope0 = slope_ref[2 * pair]
    slope1 = slope_ref[2 * pair + 1]
    q0, q1, lo = _split_pair(q_ref[...] * SCALE)
    km = km_s[:2 * SUBLANES, :].astype(BF16)
    sel0 = _rank_select_rows(_dot_nt(km, q0), own, nb - 1, n_sel)
    sel1 = _rank_select_rows(_dot_nt(km, q1), own, nb - 1, n_sel)
    row = lax.broadcasted_iota(jnp.int32, (blk, blk), 0)
    col = lax.broadcasted_iota(jnp.int32, (blk, blk), 1)
    pos = lax.broadcasted_iota(jnp.int32, (1, blk), 1).astype(F32)

    def logits(kt, first):
        p = pos + first
        return _dot(q0, kt) + slope0 * p, _dot(q1, kt) + slope1 * p

    s0, s1 = logits(kb_s[own], (own * blk).astype(F32))
    causal = col <= row
    col1 = lambda v: jnp.full((blk, 1), v, F32)
    m0, l0, m1, l1, acc = _pair_update(jnp.where(causal, s0, MASKED), jnp.where(causal, s1, MASKED), vb_s[own], lo,
                                       col1(MASKED), col1(0.0), col1(MASKED), col1(0.0), jnp.zeros((blk, PAIR), F32))
    m_s[0], l_s[0], m_s[1], l_s[1] = m0, l0, m1, l1
    acc_s[...] = acc

    for n in range(nb - 1):
        @pl.when(n < own)
        def _(n=n):
            s0, s1 = logits(kb_s[n], float(n * blk))
            s0 = jnp.where(sel0[:, n:n + 1] > 0.5, s0, MASKED)
            s1 = jnp.where(sel1[:, n:n + 1] > 0.5, s1, MASKED)
            m0, l0, m1, l1, acc = _pair_update(s0, s1, vb_s[n], lo, m_s[0], l_s[0], m_s[1], l_s[1], acc_s[...])
            m_s[0], l_s[0], m_s[1], l_s[1] = m0, l0, m1, l1
            acc_s[...] = acc

    o_ref[...] = acc_s[...] / jnp.where(lo, l_s[0], l_s[1])


def _moba_prompt(slopes, q, kt, vt, b, t):
    assert t % MOBA_BLOCK == 0
    nb = t // MOBA_BLOCK
    assert nb <= 2 * SUBLANES
    n_sel = min(MOBA_TOPK, nb - 1)
    blk = MOBA_BLOCK
    pairs = N_HEADS_A // 2
    return pl.pallas_call(
        functools.partial(_moba_prompt_kernel, nb=nb, n_sel=n_sel),
        grid=(b, pairs, nb),
        in_specs=[pl.BlockSpec(memory_space=pltpu.SMEM),
                  pl.BlockSpec((blk, PAIR), lambda bi, p, qi: (bi * nb + qi, p)),
                  pl.BlockSpec((None, PAIR, t), lambda bi, p, qi: (bi, p, 0)),
                  pl.BlockSpec((None, PAIR, t), lambda bi, p, qi: (bi, p, 0))],
        out_specs=pl.BlockSpec((blk, PAIR), lambda bi, p, qi: (bi * nb + qi, p)),
        out_shape=jax.ShapeDtypeStruct((b * t, WIDTH_A), F32),
        scratch_shapes=[pltpu.VMEM((nb, PAIR, blk), BF16), pltpu.VMEM((nb, PAIR, blk), BF16),
                        pltpu.VMEM((PAIR, LANES), F32),
                        pltpu.VMEM((2, blk, 1), F32), pltpu.VMEM((2, blk, 1), F32), pltpu.VMEM((blk, PAIR), F32)],
        compiler_params=_params("parallel", "parallel", "arbitrary"),
        name="moba_prompt",
    )(slopes, q, kt, vt)


def _sample_kernel(pt_ref, q_ref, knt_ref, vnt_ref, lfn_ref, slope_ref, *rest, n_blocks, bps, ppb, s_new, n_sel):
    del pt_ref
    npg = bps * ppb
    n_steps = n_blocks // bps
    kc, vc, lfc = rest[:npg], rest[npg:2 * npg], rest[2 * npg:3 * npg]
    o_ref, qbd_s, m_s, l_s, acc_s, km_s, tp_s = rest[3 * npg:]
    p = pl.program_id(1)
    rows = q_ref.shape[0]
    half = N_HEADS_A * rows
    page = LANES
    lane = lax.broadcasted_iota(jnp.int32, (1, LANES), 1)
    lo = lane < HEAD_DIM

    @pl.when(p == 0)
    def _():
        q = q_ref[...].astype(F32) * SCALE
        col_head = jnp.right_shift(lax.broadcasted_iota(jnp.int32, (1, q.shape[1]), 1), HEAD_DIM.bit_length() - 1)
        qbd_s[...] = jnp.concatenate([jnp.where(col_head == h, q, 0.0) for h in range(N_HEADS)], axis=0).astype(BF16)
        m_s[...] = jnp.full(m_s.shape, MASKED, F32)
        l_s[...] = jnp.zeros(l_s.shape, F32)
        km_s[...] = jnp.zeros(km_s.shape, F32)
        tp_s[...] = jnp.zeros(tp_s.shape, F32)

    def head_rows(x):
        return jnp.concatenate([jnp.broadcast_to(x[h:h + 1], (rows, x.shape[1])) for h in range(x.shape[0])], axis=0)

    def pair_rows(wide):
        return jnp.concatenate([jnp.where(lo, wide[2 * rows * j:2 * rows * j + rows],
                                          wide[2 * rows * j + rows:2 * rows * (j + 1)])
                                for j in range(N_HEADS // 2)], axis=0)

    def block_partial(kts, vts, bias, mask, slot):
        qbd = qbd_s[...]
        s = jnp.concatenate([_dot(qbd, kt) for kt in kts], axis=1) + bias
        if mask is not None:
            s = jnp.where(mask, s, MASKED)
        m = jnp.max(s, axis=-1, keepdims=True)
        e = jnp.exp(s - m)
        l = jnp.sum(e, axis=-1, keepdims=True)
        eb = e.astype(BF16)
        outs = []
        for j in range(N_HEADS // 2):
            r = None
            for a, vt in enumerate(vts):
                term = _dot_nt(eb[2 * rows * j:2 * rows * (j + 1), a * LANES:(a + 1) * LANES],
                               vt[PAIR * j:PAIR * (j + 1), :])
                r = term if r is None else r + term
            outs.append(jnp.where(lo, r[:rows], r[rows:]))
        acc_s[pl.ds(slot, 1)] = jnp.concatenate(outs, axis=0)[None]
        m_s[...] = jnp.where(lane == slot, m, m_s[...])
        l_s[...] = jnp.where(lane == slot, l, l_s[...])

    @pl.when(p < n_steps)
    def _():
        base = tp_s[...]
        ksums = None
        for bi in range(bps):
            slot = p * bps + bi
            cbs, kts, vts = [], [], []
            kacc = None
            for a in range(bi * ppb, (bi + 1) * ppb):
                pre = _lane_cumsum(lfc[a][...])
                cbs.append(-(base + pre))
                base = base + pre[:, page - 1:page]
                kf = kc[a][...]
                kts.append(kf.astype(BF16))
                vts.append(vc[a][...].astype(BF16))
                kacc = kf[:WIDTH_A] if kacc is None else kacc + kf[:WIDTH_A]
            ksum = jnp.sum(kacc, axis=-1, keepdims=True) * jnp.where(lane == slot, 1.0, 0.0)
            ksums = ksum if ksums is None else ksums + ksum
            pos = (slot * MOBA_BLOCK + lax.broadcasted_iota(jnp.int32, (1, ppb * page), 1)).astype(F32)
            bias = jnp.concatenate([slope_ref[:half, :1] * pos, head_rows(jnp.concatenate(cbs, axis=1))], axis=0)
            block_partial(kts, vts, bias, None, slot)
        tp_s[...] = base
        km_s[...] = km_s[...] + ksums

    @pl.when(p == n_steps)
    def _():
        cb = -(tp_s[...] + _lane_cumsum(lfn_ref[...]))
        pos = (n_blocks * MOBA_BLOCK + lane).astype(F32)
        bias = jnp.concatenate([slope_ref[:half, :1] * pos, head_rows(cb)], axis=0)
        nrow = N_HEADS * rows
        r_i = jnp.bitwise_and(lax.broadcasted_iota(jnp.int32, (nrow, LANES), 0), rows - 1)
        c_i = lax.broadcasted_iota(jnp.int32, (nrow, LANES), 1)
        block_partial([knt_ref[...]], [vnt_ref[...]], bias, (c_i <= r_i) & (c_i < s_new), n_blocks)

        kmean = (km_s[...] * (1.0 / MOBA_BLOCK)).astype(BF16)
        gate = _dot(qbd_s[:half, :WIDTH_A], kmean)
        sel = jnp.where(_rank_select(gate, n_blocks, n_blocks, n_sel) | (lane == n_blocks), 1.0, 0.0)
        fox = jnp.broadcast_to(jnp.where(lane <= n_blocks, 1.0, 0.0), (nrow - half, LANES))
        m_eff = jnp.where(jnp.concatenate([sel, fox], axis=0) > 0.5, m_s[...], MASKED)
        w = jnp.exp(m_eff - jnp.max(m_eff, axis=-1, keepdims=True))
        l = jnp.sum(w * l_s[...], axis=-1, keepdims=True)
        acc = jnp.zeros(acc_s.shape[1:], F32)
        for slot in range(n_blocks + 1):
            acc = acc + pair_rows(jnp.broadcast_to(w[:, slot:slot + 1], (nrow, LANES))) * acc_s[slot]
        out = acc / pair_rows(jnp.broadcast_to(l, (nrow, LANES)))
        o_ref[...] = jnp.concatenate([out[rows * j:rows * (j + 1)] for j in range(N_HEADS // 2)], axis=1)


def _sample_attention(layer, page_table, s_new, q, knt, vnt, lfn, slope_rows, cache_kt, cache_vt, cache_lft):
    bs, rows, d = q.shape
    n_pages = page_table.shape[1]
    page = cache_kt.shape[-1]
    assert page == LANES and MOBA_BLOCK % page == 0 and rows == SUBLANES
    ppb = MOBA_BLOCK // page
    assert n_pages % ppb == 0, "past length must be a multiple of the MoBA block"
    n_blocks = n_pages // ppb
    n_sel = min(MOBA_TOPK, n_blocks)
    assert 0 < n_blocks < LANES
    bps = 2 if n_blocks % 2 == 0 else 1
    n_steps = n_blocks // bps
    npg = bps * ppb

    def cache_map(a):
        return lambda b, p, pt: (layer, pt[b * n_pages + jnp.minimum(p, n_steps - 1) * npg + a], 0, 0)

    seq = lambda b, p, pt: (b, 0, 0)
    nrow = N_HEADS * rows
    grid_spec = pltpu.PrefetchScalarGridSpec(
        num_scalar_prefetch=1,
        grid=(bs, n_steps + 1),
        in_specs=[pl.BlockSpec((None, rows, d), seq),
                  pl.BlockSpec((None, d, LANES), seq),
                  pl.BlockSpec((None, d, LANES), seq),
                  pl.BlockSpec((None, N_HEADS_B, LANES), seq),
                  pl.BlockSpec((nrow, LANES), lambda b, p, pt: (0, 0))]
                 + [pl.BlockSpec((None, None, d, page), cache_map(a)) for a in range(npg)]
                 + [pl.BlockSpec((None, None, d, page), cache_map(a)) for a in range(npg)]
                 + [pl.BlockSpec((None, None, N_HEADS_B, page), cache_map(a)) for a in range(npg)],
        out_specs=pl.BlockSpec((None, rows, d), seq),
        scratch_shapes=[pltpu.VMEM((nrow, d), BF16),
                        pltpu.VMEM((nrow, LANES), F32), pltpu.VMEM((nrow, LANES), F32),
                        pltpu.VMEM((n_blocks + 1, nrow // 2, LANES), F32),
                        pltpu.VMEM((WIDTH_A, LANES), F32), pltpu.VMEM((N_HEADS_B, LANES), F32)],
    )
    return pl.pallas_call(
        functools.partial(_sample_kernel, n_blocks=n_blocks, bps=bps, ppb=ppb, s_new=s_new, n_sel=n_sel),
        grid_spec=grid_spec,
        out_shape=jax.ShapeDtypeStruct((bs, rows, d), F32),
        compiler_params=_params("parallel", "arbitrary"),
        name="sample_attention",
    )(page_table.reshape(-1), q, knt, vnt, lfn, slope_rows,
      *([cache_kt] * npg), *([cache_vt] * npg), *([cache_lft] * npg))


def _merge_kernel(oa_ref, ob_ref, g_ref, x_ref, wpa_ref, wpb_ref, wo_ref, lg_ref, lb_ref, o_ref, *, alpha):
    d = x_ref.shape[1]
    ya = _dot(oa_ref[...].astype(BF16), wpa_ref[...])
    yb = _dot(ob_ref[...].astype(BF16), wpb_ref[...])
    mix = _sigmoid(g_ref[:, :d]) * ya + _sigmoid(g_ref[:, d:]) * yb
    y = _dot(mix.astype(BF16), wo_ref[...])
    o_ref[...] = _layer_norm(alpha * x_ref[...] + y, lg_ref[...], lb_ref[...])


def _merge_norm(oa, ob, g, x, wpa, wpb, wo, lg, lb, alpha):
    n, d = x.shape
    tm = _pick_tile(n, 256)
    row = lambda i: (i, 0)
    fix = lambda i: (0, 0)
    return pl.pallas_call(
        functools.partial(_merge_kernel, alpha=alpha),
        grid=(n // tm,),
        in_specs=[pl.BlockSpec((tm, oa.shape[1]), row), pl.BlockSpec((tm, ob.shape[1]), row),
                  pl.BlockSpec((tm, g.shape[1]), row), pl.BlockSpec((tm, d), row)]
                 + [pl.BlockSpec(w.shape, fix) for w in (wpa, wpb, wo, lg, lb)],
        out_specs=pl.BlockSpec((tm, d), row),
        out_shape=jax.ShapeDtypeStruct((n, d), F32),
        compiler_params=_params("parallel"),
        name="merge_norm",
    )(oa, ob, g, x, wpa, wpb, wo, lg, lb)


def _swiglu_kernel(te_ref, nv_ref, x_ref, wg_ref, wu_ref, wd_ref, lg_ref, lb_ref, o_ref, xb_s, acc_s,
                   *, nj, fuse_norm, alpha):
    del te_ref
    i = pl.program_id(0)
    j = pl.program_id(1)
    live = i < nv_ref[0]

    @pl.when(live)
    def _():
        @pl.when(j == 0)
        def _():
            xb_s[...] = x_ref[...].astype(BF16)

        xb = xb_s[...]
        hg = _dot(xb, wg_ref[...])
        hu = _dot(xb, wu_ref[...])
        part = _dot((hg * _sigmoid(hg) * hu).astype(BF16), wd_ref[...])

        @pl.when(j == 0)
        def _():
            acc_s[...] = part

        @pl.when(j > 0)
        def _():
            acc_s[...] = acc_s[...] + part

        @pl.when(j == nj - 1)
        def _():
            if fuse_norm:
                o_ref[...] = _layer_norm(alpha * x_ref[...] + acc_s[...], lg_ref[...], lb_ref[...])
            else:
                o_ref[...] = acc_s[...]

    @pl.when(jnp.logical_not(live) & (j == nj - 1))
    def _():
        o_ref[...] = jnp.zeros(o_ref.shape, F32)


def _swiglu(tile_expert, n_live, x, wg, wu, wd, lg, lb, *, tm, tf, fuse_norm, alpha):
    n, d = x.shape
    f = wg.shape[2]
    assert n % tm == 0 and f % tf == 0
    nj = f // tf
    grid_spec = pltpu.PrefetchScalarGridSpec(
        num_scalar_prefetch=2,
        grid=(n // tm, nj),
        in_specs=[pl.BlockSpec((tm, d), lambda i, j, te, nv: (i, 0)),
                  pl.BlockSpec((None, d, tf), lambda i, j, te, nv: (te[i], 0, j)),
                  pl.BlockSpec((None, d, tf), lambda i, j, te, nv: (te[i], 0, j)),
                  pl.BlockSpec((None, tf, d), lambda i, j, te, nv: (te[i], j, 0)),
                  pl.BlockSpec((1, d), lambda i, j, te, nv: (0, 0)),
                  pl.BlockSpec((1, d), lambda i, j, te, nv: (0, 0))],
        out_specs=pl.BlockSpec((tm, d), lambda i, j, te, nv: (i, 0)),
        scratch_shapes=[pltpu.VMEM((tm, d), BF16), pltpu.VMEM((tm, d), F32)],
    )
    return pl.pallas_call(
        functools.partial(_swiglu_kernel, nj=nj, fuse_norm=fuse_norm, alpha=alpha),
        grid_spec=grid_spec,
        out_shape=jax.ShapeDtypeStruct((n, d), F32),
        compiler_params=_params("parallel", "arbitrary"),
        name="swiglu_norm" if fuse_norm else "swiglu_grouped",
    )(tile_expert, n_live, x, wg, wu, wd, lg, lb)


def _hidden_tile(f, cap):
    best = None
    for tf in range(LANES, min(f, cap) + 1, LANES):
        if f % tf == 0:
            best = tf
    assert best is not None, f
    return best


def _router_kernel(x_ref, wr_ref, br_ref, o_ref, cnt_ref, carry_s, *, n_experts):
    i = pl.program_id(0)
    tm = x_ref.shape[0]

    @pl.when(i == 0)
    def _():
        carry_s[...] = jnp.zeros(carry_s.shape, F32)

    lane = lax.broadcasted_iota(jnp.int32, (tm, LANES), 1)
    logits = _dot(x_ref[...].astype(BF16), wr_ref[...]) + br_ref[...]
    logits = jnp.where(lane < n_experts, logits, MASKED)
    v1 = jnp.max(logits, axis=-1, keepdims=True)
    i1 = jnp.min(jnp.where(logits == v1, lane, LANES), axis=-1, keepdims=True)
    rest = jnp.where(lane == i1, MASKED, logits)
    v2 = jnp.max(rest, axis=-1, keepdims=True)
    i2 = jnp.min(jnp.where(rest == v2, lane, LANES), axis=-1, keepdims=True)
    e = jnp.exp(v2 - v1)
    g1 = 1.0 / (1.0 + e)
    g2 = e / (1.0 + e)
    oh1 = jnp.where(lane == i1, 1.0, 0.0)
    oh2 = jnp.where(lane == i2, 1.0, 0.0)
    both = oh1 + oh2
    r_i = lax.broadcasted_iota(jnp.int32, (tm, tm), 0)
    c_i = lax.broadcasted_iota(jnp.int32, (tm, tm), 1)
    earlier = jnp.where(c_i < r_i, 1.0, 0.0).astype(BF16)
    before = carry_s[...] + _dot(earlier, both.astype(BF16))
    r1 = jnp.sum(oh1 * before, axis=-1, keepdims=True)
    r2 = jnp.sum(oh2 * before, axis=-1, keepdims=True)
    carry_s[...] = carry_s[...] + jnp.sum(both, axis=0, keepdims=True)
    out = jnp.where(lane == 0, i1.astype(F32), 0.0)
    out = jnp.where(lane == 1, i2.astype(F32), out)
    out = jnp.where(lane == 2, g1, out)
    out = jnp.where(lane == 3, g2, out)
    out = jnp.where(lane == 4, r1, out)
    out = jnp.where(lane == 5, r2, out)
    o_ref[...] = out
    cnt_ref[...] = carry_s[...]


def _route(x, wr, br, n_experts):
    n, d = x.shape
    tm = _pick_tile(n, 512)
    return pl.pallas_call(
        functools.partial(_router_kernel, n_experts=n_experts),
        grid=(n // tm,),
        in_specs=[pl.BlockSpec((tm, d), lambda i: (i, 0)),
                  pl.BlockSpec(wr.shape, lambda i: (0, 0)),
                  pl.BlockSpec(br.shape, lambda i: (0, 0))],
        out_specs=[pl.BlockSpec((tm, LANES), lambda i: (i, 0)), pl.BlockSpec((1, LANES), lambda i: (0, 0))],
        out_shape=[jax.ShapeDtypeStruct((n, LANES), F32), jax.ShapeDtypeStruct((1, LANES), F32)],
        scratch_shapes=[pltpu.VMEM((1, LANES), F32)],
        compiler_params=_params("arbitrary"),
        name="moe_route",
    )(x, wr, br)


def _row_copy(src, dst, sem):
    return pltpu.make_async_copy(src, dst, sem)


def _dispatch_kernel(dest_ref, x_ref, init_ref, o_ref, sem):
    del init_ref
    ts = x_ref.shape[0]

    def issue(r, c):
        for k in range(TOP_K):
            _row_copy(x_ref.at[pl.ds(r, 1)], o_ref.at[pl.ds(dest_ref[TOP_K * r + k], 1)], sem).start()
        return c

    lax.fori_loop(0, ts, issue, 0, unroll=SUBLANES)

    def drain(r, c):
        for k in range(TOP_K):
            _row_copy(x_ref.at[pl.ds(0, 1)], o_ref.at[pl.ds(0, 1)], sem).wait()
        return c

    lax.fori_loop(0, ts, drain, 0)


def _dispatch(dest, x, n_slots):
    n, d = x.shape
    ts = _pick_tile(n, 256)
    init = jnp.zeros((n_slots, d), F32)
    return pl.pallas_call(
        _dispatch_kernel,
        grid=(n // ts,),
        in_specs=[pl.BlockSpec((TOP_K * ts,), lambda i: (i,), memory_space=pltpu.SMEM),
                  pl.BlockSpec((ts, d), lambda i: (i, 0)),
                  pl.BlockSpec(memory_space=pl.ANY)],
        out_specs=pl.BlockSpec(memory_space=pl.ANY),
        out_shape=jax.ShapeDtypeStruct((n_slots, d), F32),
        scratch_shapes=[pltpu.SemaphoreType.DMA(())],
        input_output_aliases={2: 0},
        compiler_params=_params("arbitrary"),
        name="moe_dispatch",
    )(dest, x, init)


def _combine_kernel(dest_ref, r_ref, x_ref, y_ref, lg_ref, lb_ref, o_ref, buf, sem, *, alpha):
    ts = x_ref.shape[0]

    def issue(r, c):
        for k in range(TOP_K):
            _row_copy(y_ref.at[pl.ds(dest_ref[TOP_K * r + k], 1)], buf.at[k, pl.ds(r, 1)], sem).start()
        return c

    lax.fori_loop(0, ts, issue, 0, unroll=SUBLANES)

    def drain(r, c):
        for k in range(TOP_K):
            _row_copy(y_ref.at[pl.ds(0, 1)], buf.at[k, pl.ds(0, 1)], sem).wait()
        return c

    lax.fori_loop(0, ts, drain, 0)
    y = r_ref[:, 2:3] * buf[0] + r_ref[:, 3:4] * buf[1]
    o_ref[...] = _layer_norm(alpha * x_ref[...] + y, lg_ref[...], lb_ref[...])


def _combine_norm(dest, routing, x, y_sorted, lg, lb, alpha):
    n, d = x.shape
    ts = _pick_tile(n, 256)
    return pl.pallas_call(
        functools.partial(_combine_kernel, alpha=alpha),
        grid=(n // ts,),
        in_specs=[pl.BlockSpec((TOP_K * ts,), lambda i: (i,), memory_space=pltpu.SMEM),
                  pl.BlockSpec((ts, LANES), lambda i: (i, 0)),
                  pl.BlockSpec((ts, d), lambda i: (i, 0)),
                  pl.BlockSpec(memory_space=pl.ANY),
                  pl.BlockSpec((1, d), lambda i: (0, 0)),
                  pl.BlockSpec((1, d), lambda i: (0, 0))],
        out_specs=pl.BlockSpec((ts, d), lambda i: (i, 0)),
        out_shape=jax.ShapeDtypeStruct((n, d), F32),
        scratch_shapes=[pltpu.VMEM((TOP_K, ts, d), F32), pltpu.SemaphoreType.DMA(())],
        compiler_params=_params("arbitrary"),
        name="moe_combine_norm",
    )(dest, routing, x, y_sorted, lg, lb)


def _moe_norm(x, wr, br, wg, wu, wd, first_expert, n_experts, lg, lb, alpha):
    n, d = x.shape
    tm = _pick_tile(n, 512)
    routing, counts = _route(x, wr, br, n_experts)
    counts = counts[0, :n_experts].astype(jnp.int32)
    tiles = (counts + tm - 1) // tm
    tile_end = jnp.cumsum(tiles)
    group_start = (tile_end - tiles) * tm
    expert = routing[:, :TOP_K].astype(jnp.int32)
    rank = routing[:, 4:4 + TOP_K].astype(jnp.int32)
    dest = (group_start[expert] + rank).reshape(-1)
    n_tiles = (TOP_K * n) // tm + n_experts
    ended = (tile_end[None, :] <= jnp.arange(n_tiles)[:, None]).astype(jnp.int32)
    tile_expert = (first_expert + jnp.minimum(jnp.sum(ended, axis=1), n_experts - 1)).astype(jnp.int32)
    n_live = tile_end[-1:].astype(jnp.int32)
    x_sorted = _dispatch(dest, x, n_tiles * tm)
    y_sorted = _swiglu(tile_expert, n_live, x_sorted, wg, wu, wd, lg, lb,
                       tm=tm, tf=_hidden_tile(wg.shape[2], 1024), fuse_norm=False, alpha=alpha)
    return _combine_norm(dest, routing, x, y_sorted, lg, lb, alpha)


def _dense_norm(x, wg, wu, wd, index, lg, lb, alpha):
    n, d = x.shape
    tm = _pick_tile(n, 512)
    tile_expert = jnp.full((n // tm,), index, jnp.int32)
    n_live = jnp.full((1,), n // tm, jnp.int32)
    return _swiglu(tile_expert, n_live, x, wg, wu, wd, lg, lb,
                   tm=tm, tf=_hidden_tile(wg.shape[2], 1536), fuse_norm=True, alpha=alpha)


def kernel(x_prompt, x_sample, cache_k, cache_v, cache_logf, page_table, w_in, b_f, w_pa, w_pb, w_o,
           ln1_g, ln1_b, ln2_g, ln2_b, w_gate_d, w_up_d, w_down_d, w_router, b_router,
           w_gate_e, w_up_e, w_down_e):
    depth, d_model, _ = w_in.shape
    b, t, _ = x_prompt.shape
    bs, s_new, _ = x_sample.shape
    n_p = b * t
    n_s = bs * s_new
    alpha = (2.0 * depth) ** 0.25
    n_pool, page = cache_k.shape[1], cache_k.shape[2]
    rows = SUBLANES
    width = WIDTH_A + WIDTH_B
    assert s_new <= rows and d_model == width
    tq = MOBA_BLOCK
    tm = 256 if n_s % 256 == 0 else LANES

    o_qa, o_ka, o_va = 0, WIDTH_A, 2 * WIDTH_A
    o_qb = 3 * WIDTH_A
    o_kb, o_vb = o_qb + WIDTH_B, o_qb + 2 * WIDTH_B
    o_f = o_qb + 3 * WIDTH_B
    o_ga = o_f + N_HEADS_B
    o_gb = o_ga + d_model

    def cols(w, *spans):
        return jnp.concatenate([w[:, lo:lo + n] for lo, n in spans], axis=1).astype(BF16)

    slopes = 2.0 ** (-8.0 * jnp.arange(1, N_HEADS_A + 1, dtype=F32) / N_HEADS_A)
    slope_rows = jnp.broadcast_to(
        jnp.repeat(jnp.concatenate([slopes, jnp.zeros((N_HEADS_B,), F32)]), rows)[:, None], (N_HEADS * rows, LANES))
    cache_kt = jnp.transpose(cache_k, (0, 1, 3, 4, 2)).reshape(depth, n_pool, width, page)
    cache_vt = jnp.transpose(cache_v, (0, 1, 3, 4, 2)).reshape(depth, n_pool, width, page)
    cache_lft = jnp.swapaxes(cache_logf, 2, 3)
    wgd, wud, wdd = w_gate_d.astype(BF16), w_up_d.astype(BF16), w_down_d.astype(BF16)
    n_experts = w_router.shape[2]
    flat = lambda w: w.astype(BF16).reshape((-1,) + w.shape[2:])
    wge, wue, wde = flat(w_gate_e), flat(w_up_e), flat(w_down_e)

    x = jnp.concatenate([x_prompt.reshape(n_p, d_model), x_sample.reshape(n_s, d_model)], axis=0)
    ktp, vtp, ftp, kts, vts, fts = [], [], [], [], [], []
    for l in range(depth):
        wl = w_in[l]
        wq = cols(wl, (o_qa, WIDTH_A), (o_qb, WIDTH_B))
        wkt = cols(wl, (o_ka, WIDTH_A), (o_kb, WIDTH_B)).T
        wvt = cols(wl, (o_va, WIDTH_A), (o_vb, WIDTH_B)).T
        wg = cols(wl, (o_ga, d_model), (o_gb, d_model))
        wft = jnp.pad(wl[:, o_f:o_ga].T, ((0, 2 * SUBLANES - N_HEADS_B), (0, 0))).astype(BF16)
        q, g, kt_p, vt_p, ft_p, kt_s, vt_s, ft_s = _project_in(
            x, wq, wkt, wvt, wg, wft, b_f[l].reshape(N_HEADS_B, 1), b, t, n_s, tm)
        for acc, a in zip((ktp, vtp, ftp, kts, vts, fts), (kt_p, vt_p, ft_p, kt_s, vt_s, ft_s)):
            acc.append(a)

        c = _cumsum_rows(ft_p.reshape(b * N_HEADS_B, t)).reshape(b, N_HEADS_B, t // tq, tq)
        oa = _moba_prompt(slopes, q, kt_p, vt_p, b, t)
        ob = _fox_prompt(q, kt_p, vt_p, c, b, t)

        def new_tokens(a, dtype):
            a = a.reshape(a.shape[0], bs, s_new).transpose(1, 0, 2).astype(dtype)
            return jnp.pad(a, ((0, 0), (0, 0), (0, LANES - s_new)))

        q_s = jnp.pad(q[n_p:].reshape(bs, s_new, width), ((0, 0), (0, rows - s_new), (0, 0)))
        o_s = _sample_attention(l, page_table, s_new, q_s, new_tokens(kt_s, BF16), new_tokens(vt_s, BF16),
                                new_tokens(ft_s, F32), slope_rows, cache_kt, cache_vt, cache_lft)
        o_s = o_s[:, :s_new].reshape(n_s, width)
        oa = jnp.concatenate([oa, o_s[:, :WIDTH_A]], axis=0)
        ob = jnp.concatenate([ob, o_s[:, WIDTH_A:]], axis=0)

        x = _merge_norm(oa, ob, g, x, w_pa[l].astype(BF16), w_pb[l].astype(BF16), w_o[l].astype(BF16),
                        ln1_g[l].reshape(1, -1), ln1_b[l].reshape(1, -1), alpha)

        i = l // 2
        lg, lb = ln2_g[l].reshape(1, -1), ln2_b[l].reshape(1, -1)
        if l % 2 == 0:
            x = _dense_norm(x, wgd, wud, wdd, i, lg, lb, alpha)
        else:
            wr = jnp.pad(w_router[i], ((0, 0), (0, LANES - n_experts))).astype(BF16)
            br = jnp.pad(b_router[i], (0, LANES - n_experts)).reshape(1, LANES)
            x = _moe_norm(x, wr, br, wge, wue, wde, i * n_experts, n_experts, lg, lb, alpha)

    def prompt_heads(parts, heads):
        a = jnp.stack(parts).reshape(depth, b, heads, -1, t)
        return jnp.transpose(a, (0, 1, 4, 2, 3))

    def sample_heads(parts, heads):
        a = jnp.stack(parts).reshape(depth, heads, -1, bs, s_new)
        return jnp.transpose(a, (0, 3, 4, 1, 2))

    return (x[:n_p].reshape(b, t, d_model), x[n_p:].reshape(bs, s_new, d_model),
            prompt_heads(ktp, N_HEADS), prompt_heads(vtp, N_HEADS),
            prompt_heads(ftp, N_HEADS_B).reshape(depth, b, t, N_HEADS_B),
            sample_heads(kts, N_HEADS), sample_heads(vts, N_HEADS),
            sample_heads(fts, N_HEADS_B).reshape(depth, bs, s_new, N_HEADS_B))
```
